```python
import jax, jax.numpy as jnp
from jax import lax
import numpy as np

D_MODEL = 1024
BATCH = 1
SEQ = 16384
DEPTH = 2
DEC_BATCH = 8
DEC_SEQ = 16
PAST_LEN = 2048

CHUNK = 64
D_CONV = D_MODEL
CONV_W = 3
D_POOL = D_MODEL
POOL_WINDOWS = (2, 4, 8, 16)
N_POOL_GROUPS = len(POOL_WINDOWS)
POOL_GROUP = D_POOL // N_POOL_GROUPS
POOL_HIST = max(POOL_WINDOWS) - 1
N_BRANCH = 2
D_FF = 4 * D_MODEL
D_IN = 3 * D_CONV + D_POOL + N_BRANCH * D_MODEL
EPS = 1e-6

kernel_name = "hybrid_gated_conv_pool_stream_step"


def rms_norm(x, g):
    xf = x.astype(jnp.float32)
    y = xf * lax.rsqrt(jnp.mean(xf * xf, axis=-1, keepdims=True) + EPS)
    return (y * g.astype(jnp.float32)).astype(x.dtype)


def short_conv_branch(b, c, v, conv_state, conv_w):
    z = c * v
    zp = jnp.concatenate([conv_state.astype(z.dtype), z], axis=1)
    T = z.shape[1]
    y = conv_w[0] * zp[:, 0:T]
    for k in range(1, CONV_W):
        y = y + conv_w[k] * zp[:, k:k + T]
    return b * y, zp[:, -(CONV_W - 1):]


def pool_branch(p, pool_state, start, pool_w, pool_scale):
    T = p.shape[1]
    pp = jnp.concatenate([pool_state.astype(p.dtype), p], axis=1)
    ppf = pp.astype(jnp.float32)
    cs = jnp.pad(lax.cumsum(ppf, axis=1), ((0, 0), (1, 0), (0, 0)))
    pos = start + jnp.arange(T, dtype=jnp.int32)
    cur = ppf[:, POOL_HIST:]
    outs = []
    for gi, w in enumerate(POOL_WINDOWS):
        sl = slice(gi * POOL_GROUP, (gi + 1) * POOL_GROUP)
        hi = cs[:, POOL_HIST + 1:POOL_HIST + 1 + T, sl]
        lo = cs[:, POOL_HIST + 1 - w:POOL_HIST + 1 - w + T, sl]
        cnt = jnp.minimum(pos + 1, w).astype(jnp.float32)[None, :, None]
        outs.append((hi - lo) / cnt - cur[..., sl])
    d = jnp.stack(outs, axis=2).astype(p.dtype)
    y = jnp.einsum('btgc,gce->btge', d, pool_w).reshape(p.shape[0], T, D_POOL)
    return y * pool_scale, pp[:, -POOL_HIST:]


def trunk_layer(x, conv_state, pool_state, start, w_in, b_gate, conv_w, w_conv_out,
                pool_w, pool_scale, w_pool_out, w_o, g_mix_pre, g_mix_post,
                w_up, w_down, g_ffn_pre, g_ffn_post):
    h = rms_norm(x, g_mix_pre)
    proj = h @ w_in
    b, c, v, p, gates = jnp.split(
        proj, [D_CONV, 2 * D_CONV, 3 * D_CONV, 3 * D_CONV + D_POOL], axis=-1)
    gates = jax.nn.sigmoid(gates + b_gate)
    g_a, g_b = jnp.split(gates, N_BRANCH, axis=-1)
    ya, new_conv = short_conv_branch(b, c, v, conv_state, conv_w)
    yb, new_pool = pool_branch(p, pool_state, start, pool_w, pool_scale)
    merged = g_a * (ya @ w_conv_out) + g_b * (yb @ w_pool_out)
    x = x + rms_norm(merged @ w_o, g_mix_post)
    h = rms_norm(x, g_ffn_pre)
    f = jnp.square(jax.nn.relu(h @ w_up)) @ w_down
    x = x + rms_norm(f, g_ffn_post)
    return x, new_conv, new_pool


def run_trunk(x, conv_states, pool_states, start, w_in, b_gate, conv_w, w_conv_out,
              pool_w, pool_scale, w_pool_out, w_o, g_mix_pre, g_mix_post,
              w_up, w_down, g_ffn_pre, g_ffn_post):
    new_convs, new_pools = [], []
    for l in range(DEPTH):
        x, nc, npl = trunk_layer(
            x, conv_states[l], pool_states[l], start, w_in[l], b_gate[l], conv_w[l],
            w_conv_out[l], pool_w[l], pool_scale[l], w_pool_out[l], w_o[l],
            g_mix_pre[l], g_mix_post[l], w_up[l], w_down[l], g_ffn_pre[l], g_ffn_post[l])
        new_convs.append(nc)
        new_pools.append(npl)
    return x, jnp.stack(new_convs, axis=0), jnp.stack(new_pools, axis=0)


def setup_inputs(seed: int = 0) -> dict:
    key = jax.random.key(seed)
    ks = jax.random.split(key, 20)
    n = jax.random.normal
    f32 = jnp.float32
    return {
        "x_prompt": n(ks[0], (BATCH, SEQ, D_MODEL), f32),
        "x_sample": n(ks[1], (DEC_BATCH, DEC_SEQ, D_MODEL), f32),
        "cache_conv": n(ks[2], (DEPTH, DEC_BATCH, CONV_W - 1, D_CONV), f32),
        "cache_pool": n(ks[3], (DEPTH, DEC_BATCH, POOL_HIST, D_POOL), f32),
        "w_in": n(ks[4], (DEPTH, D_MODEL, D_IN), f32) * D_MODEL ** -0.5,
        "b_gate": n(ks[5], (DEPTH, N_BRANCH * D_MODEL), f32) * 0.1,
        "conv_w": n(ks[6], (DEPTH, CONV_W, D_CONV), f32) * CONV_W ** -0.5,
        "w_conv_out": n(ks[7], (DEPTH, D_CONV, D_MODEL), f32) * D_CONV ** -0.5,
        "pool_w": n(ks[8], (DEPTH, N_POOL_GROUPS, POOL_GROUP, POOL_GROUP), f32) * POOL_GROUP ** -0.5,
        "pool_scale": 1.0 + 0.1 * n(ks[9], (DEPTH, D_POOL), f32),
        "w_pool_out": n(ks[10], (DEPTH, D_POOL, D_MODEL), f32) * D_POOL ** -0.5,
        "w_o": n(ks[11], (DEPTH, D_MODEL, D_MODEL), f32) * D_MODEL ** -0.5,
        "g_mix_pre": 1.0 + 0.05 * n(ks[12], (DEPTH, D_MODEL), f32),
        "g_mix_post": 1.0 + 0.05 * n(ks[13], (DEPTH, D_MODEL), f32),
        "w_up": n(ks[14], (DEPTH, D_MODEL, D_FF), f32) * D_MODEL ** -0.5,
        "w_down": n(ks[15], (DEPTH, D_FF, D_MODEL), f32) * D_FF ** -0.5,
        "g_ffn_pre": 1.0 + 0.05 * n(ks[16], (DEPTH, D_MODEL), f32),
        "g_ffn_post": 1.0 + 0.05 * n(ks[17], (DEPTH, D_MODEL), f32),
    }


def reference(x_prompt, x_sample, cache_conv, cache_pool, w_in, b_gate, conv_w, w_conv_out,
              pool_w, pool_scale, w_pool_out, w_o, g_mix_pre, g_mix_post,
              w_up, w_down, g_ffn_pre, g_ffn_post):
    weights = (w_in, b_gate, conv_w, w_conv_out, pool_w, pool_scale, w_pool_out, w_o,
               g_mix_pre, g_mix_post, w_up, w_down, g_ffn_pre, g_ffn_post)
    zero_conv = jnp.zeros((DEPTH, x_prompt.shape[0], CONV_W - 1, D_CONV), x_prompt.dtype)
    zero_pool = jnp.zeros((DEPTH, x_prompt.shape[0], POOL_HIST, D_POOL), x_prompt.dtype)
    y_prompt, conv_state_prompt, pool_state_prompt = run_trunk(
        x_prompt, zero_conv, zero_pool, 0, *weights)
    y_sample, conv_state_sample, pool_state_sample = run_trunk(
        x_sample, cache_conv, cache_pool, PAST_LEN, *weights)
    return (y_prompt, y_sample, conv_state_prompt, pool_state_prompt,
            conv_state_sample, pool_state_sample)
```

```python
import functools

import jax
import jax.numpy as jnp
from jax import lax
from jax.experimental import pallas as pl
from jax.experimental.pallas import tpu as pltpu

EPS = 1e-6
CONV_W = 3
CONV_HIST = CONV_W - 1
POOL_WINDOWS = (2, 4, 8, 16)
POOL_HIST = max(POOL_WINDOWS) - 1
N_BRANCH = 2
PAST_LEN = 2048

SUBLANES = 8
Z_HALO = -(-CONV_HIST // SUBLANES) * SUBLANES
P_HALO = -(-POOL_HIST // SUBLANES) * SUBLANES

PROMPT_TILE_ROWS = 512
VMEM_LIMIT_BYTES = 56 * 1024 * 1024

_F32 = jnp.float32
_BF16 = jnp.bfloat16


def _rms_norm(x, g):
    ms = jnp.mean(x * x, axis=-1, keepdims=True)
    return x * lax.rsqrt(ms + EPS) * g


def _dot(a, b):
    return jnp.dot(a, b, preferred_element_type=_F32)


def _mixer_kernel(x_ref, zhist_ref, phist_ref, w_in_ref, b_gate_ref, conv_w_ref,
                  w_conv_out_ref, pool_w_ref, pool_scale_ref, w_pool_out_ref,
                  w_o_ref, g_pre_ref, g_post_ref,
                  y_ref, zstate_ref, pstate_ref, zbuf, pbuf, *, segs, seg_rows, start):
    d = x_ref.shape[-1]
    L = seg_rows
    i = pl.program_id(0)

    @pl.when(i == 0)
    def _load_history():
        zbuf[:, Z_HALO - CONV_HIST:Z_HALO, :] = zhist_ref[...]
        pbuf[:, P_HALO - POOL_HIST:P_HALO, :] = phist_ref[...]

    x = x_ref[...]
    hb = _rms_norm(x, g_pre_ref[...]).astype(_BF16)

    def proj(k):
        return _dot(hb, w_in_ref[:, k * d:(k + 1) * d])

    z = proj(1) * proj(2)
    for s in range(segs):
        zbuf[s, Z_HALO:Z_HALO + L, :] = z[s * L:(s + 1) * L]
    conv_w = conv_w_ref[...]
    ys = []
    for s in range(segs):
        y = conv_w[CONV_W - 1:CONV_W] * z[s * L:(s + 1) * L]
        for k in range(CONV_W - 1):
            off = Z_HALO - (CONV_W - 1 - k)
            y = y + conv_w[k:k + 1] * zbuf[s, off:off + L, :]
        ys.append(y)
    y = ys[0] if segs == 1 else jnp.concatenate(ys, axis=0)
    ya = (proj(0) * y).astype(_BF16)
    branch_a = _dot(ya, w_conv_out_ref[...])

    p = proj(3)
    for s in range(segs):
        pbuf[s, P_HALO:P_HALO + L, :] = p[s * L:(s + 1) * L]
    pos = start + i * L + lax.broadcasted_iota(jnp.int32, (L, 1), 0)
    group = d // len(POOL_WINDOWS)
    yb_groups = []
    for gi, w in enumerate(POOL_WINDOWS):
        lanes = slice(gi * group, (gi + 1) * group)
        inv_cnt = 1.0 / jnp.minimum(pos + 1, w).astype(_F32)
        ds = []
        for s in range(segs):
            cur = pbuf[s, P_HALO:P_HALO + L, lanes]
            acc = cur
            for k in range(1, w):
                acc = acc + pbuf[s, P_HALO - k:P_HALO - k + L, lanes]
            ds.append(acc * inv_cnt - cur)
        dg = ds[0] if segs == 1 else jnp.concatenate(ds, axis=0)
        yb_groups.append(_dot(dg.astype(_BF16), pool_w_ref[gi]))
    yb = jnp.concatenate(yb_groups, axis=-1) * pool_scale_ref[...]
    branch_b = _dot(yb.astype(_BF16), w_pool_out_ref[...])

    b_gate = b_gate_ref[...]
    g_a = jax.nn.sigmoid(proj(4) + b_gate[:, :d])
    g_b = jax.nn.sigmoid(proj(5) + b_gate[:, d:])
    merged = (g_a * branch_a + g_b * branch_b).astype(_BF16)
    o = _dot(merged, w_o_ref[...])
    y_ref[...] = x + _rms_norm(o, g_post_ref[...])

    for s in range(segs):
        z_tail = zbuf[s, Z_HALO + L - CONV_HIST:Z_HALO + L, :]
        p_tail = pbuf[s, P_HALO + L - POOL_HIST:P_HALO + L, :]
        zstate_ref[s] = z_tail
        pstate_ref[s] = p_tail
        zbuf[s, Z_HALO - CONV_HIST:Z_HALO, :] = z_tail
        pbuf[s, P_HALO - POOL_HIST:P_HALO, :] = p_tail


def _ffn_kernel(x_ref, w_up_ref, w_down_ref, g_pre_ref, g_post_ref, y_ref, *, chunk):
    x = x_ref[...]
    hb = _rms_norm(x, g_pre_ref[...]).astype(_BF16)
    d_ff = w_up_ref.shape[-1]
    f = None
    for k in range(d_ff // chunk):
        u = _dot(hb, w_up_ref[:, k * chunk:(k + 1) * chunk])
        a = jnp.square(jnp.maximum(u, 0.0)).astype(_BF16)
        part = _dot(a, w_down_ref[k * chunk:(k + 1) * chunk, :])
        f = part if f is None else f + part
    y_ref[...] = x + _rms_norm(f, g_post_ref[...])


def _const_spec(shape):
    zeros = (0,) * len(shape)
    return pl.BlockSpec(shape, lambda i: zeros, pipeline_mode=pl.Buffered(1))


def _mixer_call(x, zhist, phist, w, *, segs, seg_rows, start, name):
    rows, d = x.shape
    tile = segs * seg_rows
    n_tiles = rows // tile
    assert n_tiles * tile == rows
    assert n_tiles == 1 or segs == 1
    kern = functools.partial(_mixer_kernel, segs=segs, seg_rows=seg_rows, start=start)
    weights = (w["w_in"], w["b_gate"], w["conv_w"], w["w_conv_out"], w["pool_w"],
               w["pool_scale"], w["w_pool_out"], w["w_o"], w["g_mix_pre"], w["g_mix_post"])
    row_spec = pl.BlockSpec((tile, d), lambda i: (i, 0))
    return pl.pallas_call(
        kern,
        grid=(n_tiles,),
        in_specs=[row_spec, _const_spec(zhist.shape), _const_spec(phist.shape)]
                 + [_const_spec(a.shape) for a in weights],
        out_specs=[row_spec,
                   pl.BlockSpec(zhist.shape, lambda i: (0, 0, 0)),
                   pl.BlockSpec(phist.shape, lambda i: (0, 0, 0))],
        out_shape=[jax.ShapeDtypeStruct(x.shape, x.dtype),
                   jax.ShapeDtypeStruct(zhist.shape, x.dtype),
                   jax.ShapeDtypeStruct(phist.shape, x.dtype)],
        scratch_shapes=[pltpu.VMEM((segs, Z_HALO + seg_rows, d), _F32),
                        pltpu.VMEM((segs, P_HALO + seg_rows, d), _F32)],
        compiler_params=pltpu.CompilerParams(
            dimension_semantics=("arbitrary",), vmem_limit_bytes=VMEM_LIMIT_BYTES),
        name=name,
    )(x, zhist, phist, *weights)


def _ffn_call(x, w, *, tile, name):
    rows, d = x.shape
    n_tiles = rows // tile
    assert n_tiles * tile == rows
    weights = (w["w_up"], w["w_down"], w["g_ffn_pre"], w["g_ffn_post"])
    row_spec = pl.BlockSpec((tile, d), lambda i: (i, 0))
    return pl.pallas_call(
        functools.partial(_ffn_kernel, chunk=d),
        grid=(n_tiles,),
        in_specs=[row_spec] + [_const_spec(a.shape) for a in weights],
        out_specs=row_spec,
        out_shape=jax.ShapeDtypeStruct(x.shape, x.dtype),
        compiler_params=pltpu.CompilerParams(
            dimension_semantics=("arbitrary",), vmem_limit_bytes=VMEM_LIMIT_BYTES),
        name=name,
    )(x, *weights)


def _run_trunk(x, conv_states, pool_states, layers, *, segs, seg_rows, start, tag):
    new_convs, new_pools = [], []
    for l, w in enumerate(layers):
        x, nc, npl = _mixer_call(x, conv_states[l], pool_states[l], w, segs=segs,
                                 seg_rows=seg_rows, start=start, name=f"mixer_{tag}_l{l}")
        x = _ffn_call(x, w, tile=segs * seg_rows, name=f"ffn_{tag}_l{l}")
        new_convs.append(nc)
        new_pools.append(npl)
    return x, jnp.stack(new_convs, axis=0), jnp.stack(new_pools, axis=0)


def kernel(x_prompt, x_sample, cache_conv, cache_pool, w_in, b_gate, conv_w, w_conv_out,
           pool_w, pool_scale, w_pool_out, w_o, g_mix_pre, g_mix_post,
           w_up, w_down, g_ffn_pre, g_ffn_post):
    depth, d = g_mix_pre.shape
    batch, seq, _ = x_prompt.shape
    dec_batch, dec_seq, _ = x_sample.shape
    assert batch == 1 and seq % PROMPT_TILE_ROWS == 0
    assert dec_seq >= POOL_HIST and dec_seq % SUBLANES == 0

    layers = []
    for l in range(depth):
        layers.append(dict(
            w_in=w_in[l].astype(_BF16), b_gate=b_gate[l][None, :], conv_w=conv_w[l],
            w_conv_out=w_conv_out[l].astype(_BF16), pool_w=pool_w[l].astype(_BF16),
            pool_scale=pool_scale[l][None, :], w_pool_out=w_pool_out[l].astype(_BF16),
            w_o=w_o[l].astype(_BF16), g_mix_pre=g_mix_pre[l][None, :],
            g_mix_post=g_mix_post[l][None, :], w_up=w_up[l].astype(_BF16),
            w_down=w_down[l].astype(_BF16), g_ffn_pre=g_ffn_pre[l][None, :],
            g_ffn_post=g_ffn_post[l][None, :]))

    zero_conv = jnp.zeros((depth, batch, CONV_HIST, d), x_prompt.dtype)
    zero_pool = jnp.zeros((depth, batch, POOL_HIST, d), x_prompt.dtype)
    y_prompt, conv_p, pool_p = _run_trunk(
        x_prompt.reshape(seq, d), zero_conv, zero_pool, layers,
        segs=1, seg_rows=PROMPT_TILE_ROWS, start=0, tag="prompt")
    y_sample, conv_s, pool_s = _run_trunk(
        x_sample.reshape(dec_batch * dec_seq, d), cache_conv, cache_pool, layers,
        segs=dec_batch, seg_rows=dec_seq, start=PAST_LEN, tag="sample")
    return (y_prompt.reshape(x_prompt.shape), y_sample.reshape(x_sample.shape),
            conv_p, pool_p, conv_s, pool_s)
```

```python
import functools

import jax
import jax.numpy as jnp
from jax import lax
from jax.experimental import pallas as pl
from jax.experimental.pallas import tpu as pltpu

EPS = 1e-6
CONV_W = 3
CONV_HIST = CONV_W - 1
POOL_WINDOWS = (2, 4, 8, 16)
POOL_HIST = max(POOL_WINDOWS) - 1
N_BRANCH = 2
PAST_LEN = 2048

SUBLANES = 8
Z_HALO = -(-CONV_HIST // SUBLANES) * SUBLANES
P_HALO = -(-POOL_HIST // SUBLANES) * SUBLANES

PROMPT_SUB_ROWS = 256
MIXER_SUBTILES = 2
FFN_SUBTILES = 4
VMEM_LIMIT_BYTES = 56 * 1024 * 1024

_F32 = jnp.float32
_BF16 = jnp.bfloat16


def _rms_norm(x, g):
    ms = jnp.mean(x * x, axis=-1, keepdims=True)
    return x * lax.rsqrt(ms + EPS) * g


def _dot(a, b):
    return jnp.dot(a, b, preferred_element_type=_F32)


def _mixer_kernel(x_ref, zhist_ref, phist_ref, w_in_ref, b_gate_ref, conv_w_ref,
                  w_conv_out_ref, pool_w_ref, pool_scale_ref, w_pool_out_ref,
                  w_o_ref, g_pre_ref, g_post_ref,
                  y_ref, zstate_ref, pstate_ref, zbuf, pbuf, *, segs, seg_rows, n_sub, start):
    d = x_ref.shape[-1]
    L = seg_rows
    M = segs * L
    T = n_sub * L
    i = pl.program_id(0)
    group = d // len(POOL_WINDOWS)

    @pl.when(i == 0)
    def _load_history():
        zbuf[:, Z_HALO - CONV_HIST:Z_HALO, :] = zhist_ref[...]
        pbuf[:, P_HALO - POOL_HIST:P_HALO, :] = phist_ref[...]

    def project(r):
        hb = _rms_norm(x_ref[pl.ds(r * M, M), :], g_pre_ref[...]).astype(_BF16)
        return [_dot(hb, w_in_ref[:, k * d:(k + 1) * d]) for k in range(6)]

    def mix(r, projections):
        pb, pc, pv, pp, pga, pgb = projections
        z0, p0 = Z_HALO + r * L, P_HALO + r * L
        z = pc * pv
        for s in range(segs):
            zbuf[s, z0:z0 + L, :] = z[s * L:(s + 1) * L]
        conv_w = conv_w_ref[...]
        ys = []
        for s in range(segs):
            y = conv_w[CONV_W - 1:CONV_W] * z[s * L:(s + 1) * L]
            for k in range(CONV_W - 1):
                off = z0 - (CONV_W - 1 - k)
                y = y + conv_w[k:k + 1] * zbuf[s, off:off + L, :]
            ys.append(y)
        y = ys[0] if segs == 1 else jnp.concatenate(ys, axis=0)
        ya = (pb * y).astype(_BF16)
        for s in range(segs):
            pbuf[s, p0:p0 + L, :] = pp[s * L:(s + 1) * L]
        pos = start + i * T + r * L + lax.broadcasted_iota(jnp.int32, (L, 1), 0)
        dgs = []
        for gi, w in enumerate(POOL_WINDOWS):
            lanes = slice(gi * group, (gi + 1) * group)
            inv_cnt = 1.0 / jnp.minimum(pos + 1, w).astype(_F32)
            ds = []
            for s in range(segs):
                cur = pbuf[s, p0:p0 + L, lanes]
                acc = cur
                for k in range(1, w):
                    acc = acc + pbuf[s, p0 - k:p0 - k + L, lanes]
                ds.append(acc * inv_cnt - cur)
            dg = ds[0] if segs == 1 else jnp.concatenate(ds, axis=0)
            dgs.append(dg.astype(_BF16))
        b_gate = b_gate_ref[...]
        g_a = jax.nn.sigmoid(pga + b_gate[:, :d])
        g_b = jax.nn.sigmoid(pgb + b_gate[:, d:])
        return ya, dgs, g_a, g_b

    def output(r, ya, dgs, g_a, g_b):
        branch_a = _dot(ya, w_conv_out_ref[...])
        yb = jnp.concatenate([_dot(dg, pool_w_ref[gi]) for gi, dg in enumerate(dgs)], axis=-1)
        yb = (yb * pool_scale_ref[...]).astype(_BF16)
        branch_b = _dot(yb, w_pool_out_ref[...])
        merged = (g_a * branch_a + g_b * branch_b).astype(_BF16)
        o = _dot(merged, w_o_ref[...])
        rows = pl.ds(r * M, M)
        y_ref[rows, :] = x_ref[rows, :] + _rms_norm(o, g_post_ref[...])

    projections = project(0)
    for r in range(n_sub):
        upcoming = project(r + 1) if r + 1 < n_sub else None
        output(r, *mix(r, projections))
        projections = upcoming

    for s in range(segs):
        z_tail = zbuf[s, Z_HALO + T - CONV_HIST:Z_HALO + T, :]
        p_tail = pbuf[s, P_HALO + T - POOL_HIST:P_HALO + T, :]
        zstate_ref[s] = z_tail
        pstate_ref[s] = p_tail
        zbuf[s, Z_HALO - CONV_HIST:Z_HALO, :] = z_tail
        pbuf[s, P_HALO - POOL_HIST:P_HALO, :] = p_tail


def _ffn_kernel(x_ref, w_up_ref, w_down_ref, g_pre_ref, g_post_ref, y_ref, *, chunk, sub_rows):
    d_ff = w_up_ref.shape[-1]
    n_sub = x_ref.shape[0] // sub_rows
    n_chunks = d_ff // chunk

    def pre(r):
        return _rms_norm(x_ref[pl.ds(r * sub_rows, sub_rows), :], g_pre_ref[...]).astype(_BF16)

    def post(r, f):
        rows = pl.ds(r * sub_rows, sub_rows)
        y_ref[rows, :] = x_ref[rows, :] + _rms_norm(f, g_post_ref[...])

    hbs = [pre(r) for r in range(n_sub)]
    fs = [None] * n_sub
    for k in range(n_chunks):
        for r in range(n_sub):
            u = _dot(hbs[r], w_up_ref[:, k * chunk:(k + 1) * chunk])
            a = jnp.square(jnp.maximum(u, 0.0)).astype(_BF16)
            part = _dot(a, w_down_ref[k * chunk:(k + 1) * chunk, :])
            fs[r] = part if fs[r] is None else fs[r] + part
    for r in range(n_sub):
        post(r, fs[r])


def _const_spec(shape):
    zeros = (0,) * len(shape)
    return pl.BlockSpec(shape, lambda i: zeros, pipeline_mode=pl.Buffered(1))


def _mixer_call(x, zhist, phist, w, *, segs, seg_rows, n_sub, start, name):
    rows, d = x.shape
    tile = n_sub * segs * seg_rows
    n_tiles = rows // tile
    assert n_tiles * tile == rows
    assert (n_tiles == 1 and n_sub == 1) or segs == 1
    kern = functools.partial(_mixer_kernel, segs=segs, seg_rows=seg_rows, n_sub=n_sub,
                             start=start)
    weights = (w["w_in"], w["b_gate"], w["conv_w"], w["w_conv_out"], w["pool_w"],
               w["pool_scale"], w["w_pool_out"], w["w_o"], w["g_mix_pre"], w["g_mix_post"])
    row_spec = pl.BlockSpec((tile, d), lambda i: (i, 0))
    return pl.pallas_call(
        kern,
        grid=(n_tiles,),
        in_specs=[row_spec, _const_spec(zhist.shape), _const_spec(phist.shape)]
                 + [_const_spec(a.shape) for a in weights],
        out_specs=[row_spec,
                   pl.BlockSpec(zhist.shape, lambda i: (0, 0, 0)),
                   pl.BlockSpec(phist.shape, lambda i: (0, 0, 0))],
        out_shape=[jax.ShapeDtypeStruct(x.shape, x.dtype),
                   jax.ShapeDtypeStruct(zhist.shape, x.dtype),
                   jax.ShapeDtypeStruct(phist.shape, x.dtype)],
        scratch_shapes=[pltpu.VMEM((segs, Z_HALO + n_sub * seg_rows, d), _F32),
                        pltpu.VMEM((segs, P_HALO + n_sub * seg_rows, d), _F32)],
        compiler_params=pltpu.CompilerParams(
            dimension_semantics=("arbitrary",), vmem_limit_bytes=VMEM_LIMIT_BYTES),
        name=name,
    )(x, zhist, phist, *weights)


def _ffn_call(x, w, *, tile, sub_rows, name):
    rows, d = x.shape
    n_tiles = rows // tile
    assert n_tiles * tile == rows and tile % sub_rows == 0
    weights = (w["w_up"], w["w_down"], w["g_ffn_pre"], w["g_ffn_post"])
    row_spec = pl.BlockSpec((tile, d), lambda i: (i, 0))
    return pl.pallas_call(
        functools.partial(_ffn_kernel, chunk=d, sub_rows=sub_rows),
        grid=(n_tiles,),
        in_specs=[row_spec] + [_const_spec(a.shape) for a in weights],
        out_specs=row_spec,
        out_shape=jax.ShapeDtypeStruct(x.shape, x.dtype),
        compiler_params=pltpu.CompilerParams(
            dimension_semantics=("arbitrary",), vmem_limit_bytes=VMEM_LIMIT_BYTES),
        name=name,
    )(x, *weights)


def _run_trunk(x, conv_states, pool_states, layers, *, segs, seg_rows, mixer_sub, ffn_sub,
               start, tag):
    sub_rows = segs * seg_rows
    new_convs, new_pools = [], []
    for l, w in enumerate(layers):
        x, nc, npl = _mixer_call(x, conv_states[l], pool_states[l], w, segs=segs,
                                 seg_rows=seg_rows, n_sub=mixer_sub, start=start,
                                 name=f"mixer_{tag}_l{l}")
        x = _ffn_call(x, w, tile=ffn_sub * sub_rows, sub_rows=sub_rows,
                      name=f"ffn_{tag}_l{l}")
        new_convs.append(nc)
        new_pools.append(npl)
    return x, jnp.stack(new_convs, axis=0), jnp.stack(new_pools, axis=0)


def kernel(x_prompt, x_sample, cache_conv, cache_pool, w_in, b_gate, conv_w, w_conv_out,
           pool_w, pool_scale, w_pool_out, w_o, g_mix_pre, g_mix_post,
           w_up, w_down, g_ffn_pre, g_ffn_post):
    depth, d = g_mix_pre.shape
    batch, seq, _ = x_prompt.shape
    dec_batch, dec_seq, _ = x_sample.shape
    assert batch == 1
    assert seq % (PROMPT_SUB_ROWS * MIXER_SUBTILES) == 0
    assert seq % (PROMPT_SUB_ROWS * FFN_SUBTILES) == 0
    assert dec_seq >= POOL_HIST and dec_seq % SUBLANES == 0

    layers = []
    for l in range(depth):
        layers.append(dict(
            w_in=w_in[l].astype(_BF16), b_gate=b_gate[l][None, :], conv_w=conv_w[l],
            w_conv_out=w_conv_out[l].astype(_BF16), pool_w=pool_w[l].astype(_BF16),
            pool_scale=pool_scale[l][None, :], w_pool_out=w_pool_out[l].astype(_BF16),
            w_o=w_o[l].astype(_BF16), g_mix_pre=g_mix_pre[l][None, :],
            g_mix_post=g_mix_post[l][None, :], w_up=w_up[l].astype(_BF16),
            w_down=w_down[l].astype(_BF16), g_ffn_pre=g_ffn_pre[l][None, :],
            g_ffn_post=g_ffn_post[l][None, :]))

    zero_conv = jnp.zeros((depth, batch, CONV_HIST, d), x_prompt.dtype)
    zero_pool = jnp.zeros((depth, batch, POOL_HIST, d), x_prompt.dtype)
    y_prompt, conv_p, pool_p = _run_trunk(
        x_prompt.reshape(seq, d), zero_conv, zero_pool, layers,
        segs=1, seg_rows=PROMPT_SUB_ROWS, mixer_sub=MIXER_SUBTILES, ffn_sub=FFN_SUBTILES,
        start=0, tag="prompt")
    y_sample, conv_s, pool_s = _run_trunk(
        x_sample.reshape(dec_batch * dec_seq, d), cache_conv, cache_pool, layers,
        segs=dec_batch, seg_rows=dec_seq, mixer_sub=1, ffn_sub=1, start=PAST_LEN, tag="sample")
    return (y_prompt.reshape(x_prompt.shape), y_sample.reshape(x_sample.shape),
            conv_p, pool_p, conv_s, pool_s)
```

```python
import functools

import jax
import jax.numpy as jnp
from jax import lax
from jax.experimental import pallas as pl
from jax.experimental.pallas import tpu as pltpu

EPS = 1e-6
CONV_W = 3
CONV_HIST = CONV_W - 1
POOL_WINDOWS = (2, 4, 8, 16)
POOL_HIST = max(POOL_WINDOWS) - 1
N_PROJ = 6
PAST_LEN = 2048

SUBLANES = 8
Z_HALO = -(-CONV_HIST // SUBLANES) * SUBLANES
P_HALO = -(-POOL_HIST // SUBLANES) * SUBLANES

PROMPT_SUB_ROWS = 256
MIXER_SUBTILES = 2
FFN_SUBTILES = 4
WEIGHT_STAGE_BYTES = 2 * 1024 * 1024
VMEM_LIMIT_BYTES = 56 * 1024 * 1024

_F32 = jnp.float32
_BF16 = jnp.bfloat16


def _rms_norm(x, g):
    ms = jnp.mean(x * x, axis=-1, keepdims=True)
    return x * lax.rsqrt(ms + EPS) * g


def _dot(a, b):
    return jnp.dot(a, b, preferred_element_type=_F32)


def _stage_rows(n_rows, n_cols):
    rows = n_rows
    while rows * n_cols * 4 > WEIGHT_STAGE_BYTES and rows % (2 * SUBLANES) == 0:
        rows //= 2
    return rows


def _load_weights_as_bf16(pairs):
    shapes = []
    for src, _ in pairs:
        shape = (_stage_rows(*src.shape), src.shape[1])
        if shape not in shapes:
            shapes.append(shape)

    def body(*scratch):
        stages, sems = scratch[:len(shapes)], scratch[len(shapes):]
        used = [0] * len(shapes)
        jobs = []
        for src, dst in pairs:
            rows = _stage_rows(*src.shape)
            which = shapes.index((rows, src.shape[1]))
            for k in range(src.shape[0] // rows):
                slot = used[which] % 2
                used[which] += 1
                chunk = pl.ds(k * rows, rows)
                copy = pltpu.make_async_copy(
                    src.at[chunk], stages[which].at[slot], sems[which].at[slot])
                jobs.append((copy, stages[which], slot, dst, chunk))
        jobs[0][0].start()
        for j, (copy, stage, slot, dst, chunk) in enumerate(jobs):
            if j + 1 < len(jobs):
                jobs[j + 1][0].start()
            copy.wait()
            dst[chunk, :] = stage[slot].astype(_BF16)

    pl.run_scoped(body,
                  *[pltpu.VMEM((2,) + shape, _F32) for shape in shapes],
                  *[pltpu.SemaphoreType.DMA((2,)) for _ in shapes])


def _mixer_tile(x_ref, y_ref, zstate_ref, pstate_ref, zbuf, pbuf, w, *,
                tile_index, segs, seg_rows, n_sub, start):
    d = x_ref.shape[-1]
    L = seg_rows
    M = segs * L
    T = n_sub * L
    group = d // len(POOL_WINDOWS)

    def project(r):
        hb = _rms_norm(x_ref[pl.ds(r * M, M), :], w["g_pre"]).astype(_BF16)
        return [_dot(hb, w["w_in"][:, k * d:(k + 1) * d]) for k in range(N_PROJ)]

    def mix(r, projections):
        pb, pc, pv, pp, pga, pgb = projections
        z0, p0 = Z_HALO + r * L, P_HALO + r * L
        z = pc * pv
        for s in range(segs):
            zbuf[s, z0:z0 + L, :] = z[s * L:(s + 1) * L]
        conv_w = w["conv_w"]
        ys = []
        for s in range(segs):
            y = conv_w[CONV_W - 1:CONV_W] * z[s * L:(s + 1) * L]
            for k in range(CONV_W - 1):
                off = z0 - (CONV_W - 1 - k)
                y = y + conv_w[k:k + 1] * zbuf[s, off:off + L, :]
            ys.append(y)
        y = ys[0] if segs == 1 else jnp.concatenate(ys, axis=0)
        ya = (pb * y).astype(_BF16)
        for s in range(segs):
            pbuf[s, p0:p0 + L, :] = pp[s * L:(s + 1) * L]
        pos = start + tile_index * T + r * L + lax.broadcasted_iota(jnp.int32, (L, 1), 0)
        dgs = []
        for gi, win in enumerate(POOL_WINDOWS):
            lanes = slice(gi * group, (gi + 1) * group)
            inv_cnt = 1.0 / jnp.minimum(pos + 1, win).astype(_F32)
            ds = []
            for s in range(segs):
                cur = pbuf[s, p0:p0 + L, lanes]
                acc = cur
                for k in range(1, win):
                    acc = acc + pbuf[s, p0 - k:p0 - k + L, lanes]
                ds.append(acc * inv_cnt - cur)
            dg = ds[0] if segs == 1 else jnp.concatenate(ds, axis=0)
            dgs.append(dg.astype(_BF16))
        b_gate = w["b_gate"]
        g_a = jax.nn.sigmoid(pga + b_gate[:, :d])
        g_b = jax.nn.sigmoid(pgb + b_gate[:, d:])
        return ya, dgs, g_a, g_b

    def output(r, ya, dgs, g_a, g_b):
        branch_a = _dot(ya, w["w_conv_out"][...])
        yb = jnp.concatenate([_dot(dg, w["pool_w"][gi][...]) for gi, dg in enumerate(dgs)],
                             axis=-1)
        yb = (yb * w["pool_scale"]).astype(_BF16)
        branch_b = _dot(yb, w["w_pool_out"][...])
        merged = (g_a * branch_a + g_b * branch_b).astype(_BF16)
        o = _dot(merged, w["w_o"][...])
        rows = pl.ds(r * M, M)
        y_ref[rows, :] = x_ref[rows, :] + _rms_norm(o, w["g_post"])

    projections = project(0)
    for r in range(n_sub):
        upcoming = project(r + 1) if r + 1 < n_sub else None
        output(r, *mix(r, projections))
        projections = upcoming

    for s in range(segs):
        z_tail = zbuf[s, Z_HALO + T - CONV_HIST:Z_HALO + T, :]
        p_tail = pbuf[s, P_HALO + T - POOL_HIST:P_HALO + T, :]
        zstate_ref[s] = z_tail
        pstate_ref[s] = p_tail
        zbuf[s, Z_HALO - CONV_HIST:Z_HALO, :] = z_tail
        pbuf[s, P_HALO - POOL_HIST:P_HALO, :] = p_tail


def _mixer_kernel(xp_ref, xs_ref, zhist_ref, phist_ref, b_gate_ref, conv_w_ref,
                  pool_scale_ref, g_pre_ref, g_post_ref,
                  w_in_hbm, w_conv_out_hbm, pool_w_hbm, w_pool_out_hbm, w_o_hbm,
                  yp_ref, ys_ref, zstate_p_ref, pstate_p_ref, zstate_s_ref, pstate_s_ref,
                  w_in_v, w_conv_out_v, pool_w_v, w_pool_out_v, w_o_v,
                  zbuf_p, pbuf_p, zbuf_s, pbuf_s, *, layer, n_prompt_tiles, dec_rows):
    i = pl.program_id(0)
    group = pool_w_v.shape[-1]

    @pl.when(i == 0)
    def _first_step():
        _load_weights_as_bf16([
            (w_in_hbm.at[layer], w_in_v),
            (w_conv_out_hbm.at[layer], w_conv_out_v),
            (pool_w_hbm.at[layer], pool_w_v),
            (w_pool_out_hbm.at[layer], w_pool_out_v),
            (w_o_hbm.at[layer], w_o_v)])
        zbuf_p[:, 0:Z_HALO, :] = jnp.zeros((zbuf_p.shape[0], Z_HALO, zbuf_p.shape[2]), _F32)
        pbuf_p[:, 0:P_HALO, :] = jnp.zeros((pbuf_p.shape[0], P_HALO, pbuf_p.shape[2]), _F32)

    w = dict(
        w_in=w_in_v, w_conv_out=w_conv_out_v, w_pool_out=w_pool_out_v, w_o=w_o_v,
        pool_w=[pool_w_v.at[pl.ds(g * group, group)] for g in range(len(POOL_WINDOWS))],
        b_gate=b_gate_ref[layer:layer + 1, :], conv_w=conv_w_ref[layer],
        pool_scale=pool_scale_ref[layer:layer + 1, :],
        g_pre=g_pre_ref[layer:layer + 1, :], g_post=g_post_ref[layer:layer + 1, :])

    @pl.when(i < n_prompt_tiles)
    def _prompt_tile():
        _mixer_tile(xp_ref, yp_ref, zstate_p_ref, pstate_p_ref, zbuf_p, pbuf_p, w,
                    tile_index=i, segs=1, seg_rows=PROMPT_SUB_ROWS, n_sub=MIXER_SUBTILES,
                    start=0)

    @pl.when(i == n_prompt_tiles)
    def _decode_rows():
        zbuf_s[:, Z_HALO - CONV_HIST:Z_HALO, :] = zhist_ref[...]
        pbuf_s[:, P_HALO - POOL_HIST:P_HALO, :] = phist_ref[...]
        _mixer_tile(xs_ref, ys_ref, zstate_s_ref, pstate_s_ref, zbuf_s, pbuf_s, w,
                    tile_index=0, segs=zhist_ref.shape[0], seg_rows=dec_rows, n_sub=1,
                    start=PAST_LEN)


def _ffn_rows(x_ref, y_ref, w_up, w_down, g_pre, g_post, *, sub_rows, chunk):
    n_sub = x_ref.shape[0] // sub_rows
    n_chunks = w_up.shape[-1] // chunk
    hbs = [_rms_norm(x_ref[pl.ds(r * sub_rows, sub_rows), :], g_pre).astype(_BF16)
           for r in range(n_sub)]
    fs = [None] * n_sub
    for k in range(n_chunks):
        for r in range(n_sub):
            u = _dot(hbs[r], w_up[:, k * chunk:(k + 1) * chunk])
            a = jnp.square(jnp.maximum(u, 0.0)).astype(_BF16)
            part = _dot(a, w_down[k * chunk:(k + 1) * chunk, :])
            fs[r] = part if fs[r] is None else fs[r] + part
    for r in range(n_sub):
        rows = pl.ds(r * sub_rows, sub_rows)
        y_ref[rows, :] = x_ref[rows, :] + _rms_norm(fs[r], g_post)


def _ffn_kernel(xp_ref, xs_ref, g_pre_ref, g_post_ref, w_up_hbm, w_down_hbm,
                yp_ref, ys_ref, w_up_v, w_down_v, *, layer, n_prompt_tiles):
    i = pl.program_id(0)
    d = xp_ref.shape[-1]

    @pl.when(i == 0)
    def _first_step():
        _load_weights_as_bf16([(w_up_hbm.at[layer], w_up_v), (w_down_hbm.at[layer], w_down_v)])

    g_pre = g_pre_ref[layer:layer + 1, :]
    g_post = g_post_ref[layer:layer + 1, :]

    @pl.when(i < n_prompt_tiles)
    def _prompt_tile():
        _ffn_rows(xp_ref, yp_ref, w_up_v, w_down_v, g_pre, g_post,
                  sub_rows=PROMPT_SUB_ROWS, chunk=d)

    @pl.when(i == n_prompt_tiles)
    def _decode_rows():
        _ffn_rows(xs_ref, ys_ref, w_up_v, w_down_v, g_pre, g_post,
                  sub_rows=xs_ref.shape[0], chunk=d)


def _full_spec(shape):
    zeros = (0,) * len(shape)
    return pl.BlockSpec(shape, lambda i: zeros)


_ANY_SPEC = pl.BlockSpec(memory_space=pl.ANY)
_COMPILER_PARAMS = pltpu.CompilerParams(
    dimension_semantics=("arbitrary",), vmem_limit_bytes=VMEM_LIMIT_BYTES)


def _mixer_call(xp, xs, zhist, phist, p, *, layer, dec_rows):
    seq, d = xp.shape
    tile = MIXER_SUBTILES * PROMPT_SUB_ROWS
    n_tiles = seq // tile
    segs = zhist.shape[0]
    small = (p["b_gate"], p["conv_w"], p["pool_scale"], p["g_mix_pre"], p["g_mix_post"])
    big = (p["w_in"], p["w_conv_out"], p["pool_w"], p["w_pool_out"], p["w_o"])
    prompt_spec = pl.BlockSpec((tile, d), lambda i: (jnp.minimum(i, n_tiles - 1), 0))
    f32 = xp.dtype
    kern = functools.partial(_mixer_kernel, layer=layer, n_prompt_tiles=n_tiles,
                             dec_rows=dec_rows)
    return pl.pallas_call(
        kern,
        grid=(n_tiles + 1,),
        in_specs=[prompt_spec, _full_spec(xs.shape), _full_spec(zhist.shape),
                  _full_spec(phist.shape)]
                 + [_full_spec(a.shape) for a in small] + [_ANY_SPEC] * len(big),
        out_specs=[prompt_spec, _full_spec(xs.shape),
                   _full_spec((1, CONV_HIST, d)), _full_spec((1, POOL_HIST, d)),
                   _full_spec(zhist.shape), _full_spec(phist.shape)],
        out_shape=[jax.ShapeDtypeStruct(xp.shape, f32), jax.ShapeDtypeStruct(xs.shape, f32),
                   jax.ShapeDtypeStruct((1, CONV_HIST, d), f32),
                   jax.ShapeDtypeStruct((1, POOL_HIST, d), f32),
                   jax.ShapeDtypeStruct(zhist.shape, f32),
                   jax.ShapeDtypeStruct(phist.shape, f32)],
        scratch_shapes=[pltpu.VMEM(a.shape[1:], _BF16) for a in big]
                       + [pltpu.VMEM((1, Z_HALO + tile, d), _F32),
                          pltpu.VMEM((1, P_HALO + tile, d), _F32),
                          pltpu.VMEM((segs, Z_HALO + dec_rows, d), _F32),
                          pltpu.VMEM((segs, P_HALO + dec_rows, d), _F32)],
        compiler_params=_COMPILER_PARAMS,
        name=f"mixer_l{layer}",
    )(xp, xs, zhist, phist, *small, *big)


def _ffn_call(xp, xs, p, *, layer):
    seq, d = xp.shape
    tile = FFN_SUBTILES * PROMPT_SUB_ROWS
    n_tiles = seq // tile
    small = (p["g_ffn_pre"], p["g_ffn_post"])
    big = (p["w_up"], p["w_down"])
    prompt_spec = pl.BlockSpec((tile, d), lambda i: (jnp.minimum(i, n_tiles - 1), 0))
    return pl.pallas_call(
        functools.partial(_ffn_kernel, layer=layer, n_prompt_tiles=n_tiles),
        grid=(n_tiles + 1,),
        in_specs=[prompt_spec, _full_spec(xs.shape)]
                 + [_full_spec(a.shape) for a in small] + [_ANY_SPEC] * len(big),
        out_specs=[prompt_spec, _full_spec(xs.shape)],
        out_shape=[jax.ShapeDtypeStruct(xp.shape, xp.dtype),
                   jax.ShapeDtypeStruct(xs.shape, xs.dtype)],
        scratch_shapes=[pltpu.VMEM(a.shape[1:], _BF16) for a in big],
        compiler_params=_COMPILER_PARAMS,
        name=f"ffn_l{layer}",
    )(xp, xs, *small, *big)


def kernel(x_prompt, x_sample, cache_conv, cache_pool, w_in, b_gate, conv_w, w_conv_out,
           pool_w, pool_scale, w_pool_out, w_o, g_mix_pre, g_mix_post,
           w_up, w_down, g_ffn_pre, g_ffn_post):
    depth, d = g_mix_pre.shape
    batch, seq, _ = x_prompt.shape
    dec_batch, dec_seq, _ = x_sample.shape
    assert batch == 1
    assert seq % (PROMPT_SUB_ROWS * MIXER_SUBTILES) == 0
    assert seq % (PROMPT_SUB_ROWS * FFN_SUBTILES) == 0
    assert dec_seq >= POOL_HIST and dec_seq % SUBLANES == 0

    n_groups, group = pool_w.shape[1], pool_w.shape[2]
    params = dict(
        w_in=w_in, b_gate=b_gate, conv_w=conv_w, w_conv_out=w_conv_out,
        pool_w=pool_w.reshape(depth, n_groups * group, group), pool_scale=pool_scale,
        w_pool_out=w_pool_out, w_o=w_o, g_mix_pre=g_mix_pre, g_mix_post=g_mix_post,
        w_up=w_up, w_down=w_down, g_ffn_pre=g_ffn_pre, g_ffn_post=g_ffn_post)

    xp = x_prompt.reshape(seq, d)
    xs = x_sample.reshape(dec_batch * dec_seq, d)
    conv_p, pool_p, conv_s, pool_s = [], [], [], []
    for l in range(depth):
        xp, xs, zp, pp, zs, ps = _mixer_call(xp, xs, cache_conv[l], cache_pool[l], params,
                                             layer=l, dec_rows=dec_seq)
        xp, xs = _ffn_call(xp, xs, params, layer=l)
        conv_p.append(zp)
        pool_p.append(pp)
        conv_s.append(zs)
        pool_s.append(ps)
    return (xp.reshape(x_prompt.shape), xs.reshape(x_sample.shape),
            jnp.stack(conv_p), jnp.stack(pool_p), jnp.stack(conv_s), jnp.stack(pool_s))
```

```python
import functools

import jax
import jax.numpy as jnp
from jax import lax
from jax.experimental import pallas as pl
from jax.experimental.pallas import tpu as pltpu

EPS = 1e-6
CONV_W = 3
CONV_HIST = CONV_W - 1
POOL_WINDOWS = (2, 4, 8, 16)
POOL_HIST = max(POOL_WINDOWS) - 1
N_PROJ = 6
PAST_LEN = 2048

SUBLANES = 8
Z_HALO = -(-CONV_HIST // SUBLANES) * SUBLANES
POOL_CARRY = -(-max(POOL_WINDOWS) // SUBLANES) * SUBLANES
P_HALO = POOL_CARRY + SUBLANES
assert all(win == 2 ** (k + 1) for k, win in enumerate(POOL_WINDOWS))
assert POOL_HIST <= POOL_CARRY

PROMPT_SUB_ROWS = 256
MIXER_SUBTILES = 2
FFN_SUBTILES = 4
WEIGHT_STAGE_BYTES = 1024 * 1024
WEIGHT_COPIES_IN_FLIGHT = 3
VMEM_LIMIT_BYTES = 56 * 1024 * 1024

_F32 = jnp.float32
_BF16 = jnp.bfloat16


def _rms_norm(x, g):
    ms = jnp.mean(x * x, axis=-1, keepdims=True)
    return x * lax.rsqrt(ms + EPS) * g


def _dot(a, b):
    return jnp.dot(a, b, preferred_element_type=_F32)


def _stage_rows(n_rows, n_cols):
    rows = n_rows
    while rows * n_cols * 4 > WEIGHT_STAGE_BYTES and rows % (2 * SUBLANES) == 0:
        rows //= 2
    return rows


def _load_weights_as_bf16(pairs):
    plan = []
    for src, _ in pairs:
        shape = (_stage_rows(*src.shape), src.shape[1])
        n_chunks = src.shape[0] // shape[0]
        for k, (other, count) in enumerate(plan):
            if other == shape:
                plan[k] = (shape, count + n_chunks)
                break
        else:
            plan.append((shape, n_chunks))
    shapes = [shape for shape, _ in plan]
    slots = [min(WEIGHT_COPIES_IN_FLIGHT + 1, count) for _, count in plan]

    def body(*scratch):
        stages, sems = scratch[:len(shapes)], scratch[len(shapes):]
        used = [0] * len(shapes)
        jobs = []
        for src, dst in pairs:
            rows = _stage_rows(*src.shape)
            which = shapes.index((rows, src.shape[1]))
            for k in range(src.shape[0] // rows):
                slot = used[which] % slots[which]
                used[which] += 1
                chunk = pl.ds(k * rows, rows)
                copy = pltpu.make_async_copy(
                    src.at[chunk], stages[which].at[slot], sems[which].at[slot])
                jobs.append((copy, stages[which], slot, dst, chunk))
        for copy, *_ in jobs[:WEIGHT_COPIES_IN_FLIGHT]:
            copy.start()
        for j, (copy, stage, slot, dst, chunk) in enumerate(jobs):
            if j + WEIGHT_COPIES_IN_FLIGHT < len(jobs):
                jobs[j + WEIGHT_COPIES_IN_FLIGHT][0].start()
            copy.wait()
            dst[chunk, :] = stage[slot].astype(_BF16)

    pl.run_scoped(body,
                  *[pltpu.VMEM((n,) + shape, _F32) for n, shape in zip(slots, shapes)],
                  *[pltpu.SemaphoreType.DMA((n,)) for n in slots])


def _pool_window_sums(levels, s, row0, n):
    group = levels[0].shape[-1] // len(POOL_WINDOWS)
    acc = levels[0][s, row0:row0 + n, :]
    sums = []
    for k, win in enumerate(POOL_WINDOWS):
        shift = win // 2
        acc = acc + levels[k][s, row0 - shift:row0 - shift + n, :]
        sums.append(acc[:, :group])
        if k + 1 < len(POOL_WINDOWS):
            acc = acc[:, group:]
            levels[k + 1][s, row0:row0 + n, :] = acc
    return sums


def _mixer_tile(x_ref, y_ref, zstate_ref, pstate_ref, zbuf, levels, w, *,
                tile_index, segs, seg_rows, n_sub, start):
    d = x_ref.shape[-1]
    L = seg_rows
    M = segs * L
    T = n_sub * L
    group = d // len(POOL_WINDOWS)

    def project(r):
        hb = _rms_norm(x_ref[pl.ds(r * M, M), :], w["g_pre"]).astype(_BF16)
        return [_dot(hb, w["w_in"][:, k * d:(k + 1) * d]) for k in range(N_PROJ)]

    def mix(r, projections):
        pb, pc, pv, pp, pga, pgb = projections
        z0, p0 = Z_HALO + r * L, P_HALO + r * L
        z = pc * pv
        for s in range(segs):
            zbuf[s, z0:z0 + L, :] = z[s * L:(s + 1) * L]
        conv_w = w["conv_w"]
        ys = []
        for s in range(segs):
            y = conv_w[CONV_W - 1:CONV_W] * z[s * L:(s + 1) * L]
            for k in range(CONV_W - 1):
                off = z0 - (CONV_W - 1 - k)
                y = y + conv_w[k:k + 1] * zbuf[s, off:off + L, :]
            ys.append(y)
        y = ys[0] if segs == 1 else jnp.concatenate(ys, axis=0)
        ya = (pb * y).astype(_BF16)
        for s in range(segs):
            levels[0][s, p0:p0 + L, :] = pp[s * L:(s + 1) * L]
        sums = [_pool_window_sums(levels, s, p0, L) for s in range(segs)]
        pos = start + tile_index * T + r * L + lax.broadcasted_iota(jnp.int32, (L, 1), 0)
        dgs = []
        for gi, win in enumerate(POOL_WINDOWS):
            lanes = slice(gi * group, (gi + 1) * group)
            inv_cnt = 1.0 / jnp.minimum(pos + 1, win).astype(_F32)
            ds = [sums[s][gi] * inv_cnt - pp[s * L:(s + 1) * L, lanes] for s in range(segs)]
            dg = ds[0] if segs == 1 else jnp.concatenate(ds, axis=0)
            dgs.append(dg.astype(_BF16))
        b_gate = w["b_gate"]
        g_a = jax.nn.sigmoid(pga + b_gate[:, :d])
        g_b = jax.nn.sigmoid(pgb + b_gate[:, d:])
        return ya, dgs, g_a, g_b

    def output(r, ya, dgs, g_a, g_b):
        branch_a = _dot(ya, w["w_conv_out"][...])
        yb = jnp.concatenate([_dot(dg, w["pool_w"][gi][...]) for gi, dg in enumerate(dgs)],
                             axis=-1)
        yb = (yb * w["pool_scale"]).astype(_BF16)
        branch_b = _dot(yb, w["w_pool_out"][...])
        merged = (g_a * branch_a + g_b * branch_b).astype(_BF16)
        o = _dot(merged, w["w_o"][...])
        rows = pl.ds(r * M, M)
        y_ref[rows, :] = x_ref[rows, :] + _rms_norm(o, w["g_post"])

    projections = project(0)
    for r in range(n_sub):
        upcoming = project(r + 1) if r + 1 < n_sub else None
        output(r, *mix(r, projections))
        projections = upcoming

    for s in range(segs):
        z_tail = zbuf[s, Z_HALO + T - CONV_HIST:Z_HALO + T, :]
        zstate_ref[s] = z_tail
        zbuf[s, Z_HALO - CONV_HIST:Z_HALO, :] = z_tail
        pstate_ref[s] = levels[0][s, P_HALO + T - POOL_HIST:P_HALO + T, :]
        for buf in levels:
            buf[s, P_HALO - POOL_CARRY:P_HALO, :] = buf[s, P_HALO + T - POOL_CARRY:P_HALO + T, :]


def _mixer_kernel(xp_ref, xs_ref, zhist_ref, phist_ref, b_gate_ref, conv_w_ref,
                  pool_scale_ref, g_pre_ref, g_post_ref,
                  w_in_hbm, w_conv_out_hbm, pool_w_hbm, w_pool_out_hbm, w_o_hbm,
                  yp_ref, ys_ref, zstate_p_ref, pstate_p_ref, zstate_s_ref, pstate_s_ref,
                  w_in_v, w_conv_out_v, pool_w_v, w_pool_out_v, w_o_v,
                  zbuf_p, zbuf_s, *level_bufs, layer, n_prompt_tiles, dec_rows):
    i = pl.program_id(0)
    group = pool_w_v.shape[-1]
    levels_p = level_bufs[:len(POOL_WINDOWS)]
    levels_s = level_bufs[len(POOL_WINDOWS):]

    def zero_rows(buf, n):
        buf[:, 0:n, :] = jnp.zeros((buf.shape[0], n, buf.shape[2]), _F32)

    @pl.when(i == 0)
    def _first_step():
        _load_weights_as_bf16([
            (w_in_hbm.at[layer], w_in_v),
            (w_conv_out_hbm.at[layer], w_conv_out_v),
            (pool_w_hbm.at[layer], pool_w_v),
            (w_pool_out_hbm.at[layer], w_pool_out_v),
            (w_o_hbm.at[layer], w_o_v)])
        zero_rows(zbuf_p, Z_HALO)
        for buf in levels_p:
            zero_rows(buf, P_HALO)

    w = dict(
        w_in=w_in_v, w_conv_out=w_conv_out_v, w_pool_out=w_pool_out_v, w_o=w_o_v,
        pool_w=[pool_w_v.at[pl.ds(g * group, group)] for g in range(len(POOL_WINDOWS))],
        b_gate=b_gate_ref[layer:layer + 1, :], conv_w=conv_w_ref[layer],
        pool_scale=pool_scale_ref[layer:layer + 1, :],
        g_pre=g_pre_ref[layer:layer + 1, :], g_post=g_post_ref[layer:layer + 1, :])

    @pl.when(i < n_prompt_tiles)
    def _prompt_tile():
        _mixer_tile(xp_ref, yp_ref, zstate_p_ref, pstate_p_ref, zbuf_p, levels_p, w,
                    tile_index=i, segs=1, seg_rows=PROMPT_SUB_ROWS, n_sub=MIXER_SUBTILES,
                    start=0)

    @pl.when(i == n_prompt_tiles)
    def _decode_rows():
        segs = zhist_ref.shape[0]
        zbuf_s[:, Z_HALO - CONV_HIST:Z_HALO, :] = zhist_ref[...]
        for buf in levels_s:
            zero_rows(buf, P_HALO)
        levels_s[0][:, P_HALO - POOL_HIST:P_HALO, :] = phist_ref[...]
        for s in range(segs):
            _pool_window_sums(levels_s, s, P_HALO - POOL_CARRY, POOL_CARRY)
        _mixer_tile(xs_ref, ys_ref, zstate_s_ref, pstate_s_ref, zbuf_s, levels_s, w,
                    tile_index=0, segs=segs, seg_rows=dec_rows, n_sub=1, start=PAST_LEN)


def _ffn_rows(x_ref, y_ref, w_up, w_down, g_pre, g_post, *, sub_rows, chunk):
    n_sub = x_ref.shape[0] // sub_rows
    n_chunks = w_up.shape[-1] // chunk
    hbs = [_rms_norm(x_ref[pl.ds(r * sub_rows, sub_rows), :], g_pre).astype(_BF16)
           for r in range(n_sub)]
    fs = [None] * n_sub
    for k in range(n_chunks):
        for r in range(n_sub):
            u = _dot(hbs[r], w_up[:, k * chunk:(k + 1) * chunk])
            a = jnp.square(jnp.maximum(u, 0.0)).astype(_BF16)
            part = _dot(a, w_down[k * chunk:(k + 1) * chunk, :])
            fs[r] = part if fs[r] is None else fs[r] + part
    for r in range(n_sub):
        rows = pl.ds(r * sub_rows, sub_rows)
        y_ref[rows, :] = x_ref[rows, :] + _rms_norm(fs[r], g_post)


def _ffn_kernel(xp_ref, xs_ref, g_pre_ref, g_post_ref, w_up_hbm, w_down_hbm,
                yp_ref, ys_ref, w_up_v, w_down_v, *, layer, n_prompt_tiles):
    i = pl.program_id(0)
    d = xp_ref.shape[-1]

    @pl.when(i == 0)
    def _first_step():
        _load_weights_as_bf16([(w_up_hbm.at[layer], w_up_v), (w_down_hbm.at[layer], w_down_v)])

    g_pre = g_pre_ref[layer:layer + 1, :]
    g_post = g_post_ref[layer:layer + 1, :]

    @pl.when(i < n_prompt_tiles)
    def _prompt_tile():
        _ffn_rows(xp_ref, yp_ref, w_up_v, w_down_v, g_pre, g_post,
                  sub_rows=PROMPT_SUB_ROWS, chunk=d)

    @pl.when(i == n_prompt_tiles)
    def _decode_rows():
        _ffn_rows(xs_ref, ys_ref, w_up_v, w_down_v, g_pre, g_post,
                  sub_rows=xs_ref.shape[0], chunk=d)


def _full_spec(shape):
    zeros = (0,) * len(shape)
    return pl.BlockSpec(shape, lambda i: zeros)


_ANY_SPEC = pl.BlockSpec(memory_space=pl.ANY)
_COMPILER_PARAMS = pltpu.CompilerParams(
    dimension_semantics=("arbitrary",), vmem_limit_bytes=VMEM_LIMIT_BYTES)


def _mixer_call(xp, xs, zhist, phist, p, *, layer, dec_rows):
    seq, d = xp.shape
    tile = MIXER_SUBTILES * PROMPT_SUB_ROWS
    n_tiles = seq // tile
    segs = zhist.shape[0]
    group = d // len(POOL_WINDOWS)
    small = (p["b_gate"], p["conv_w"], p["pool_scale"], p["g_mix_pre"], p["g_mix_post"])
    big = (p["w_in"], p["w_conv_out"], p["pool_w"], p["w_pool_out"], p["w_o"])
    prompt_spec = pl.BlockSpec((tile, d), lambda i: (jnp.minimum(i, n_tiles - 1), 0))
    f32 = xp.dtype
    kern = functools.partial(_mixer_kernel, layer=layer, n_prompt_tiles=n_tiles,
                             dec_rows=dec_rows)
    return pl.pallas_call(
        kern,
        grid=(n_tiles + 1,),
        in_specs=[prompt_spec, _full_spec(xs.shape), _full_spec(zhist.shape),
                  _full_spec(phist.shape)]
                 + [_full_spec(a.shape) for a in small] + [_ANY_SPEC] * len(big),
        out_specs=[prompt_spec, _full_spec(xs.shape),
                   _full_spec((1, CONV_HIST, d)), _full_spec((1, POOL_HIST, d)),
                   _full_spec(zhist.shape), _full_spec(phist.shape)],
        out_shape=[jax.ShapeDtypeStruct(xp.shape, f32), jax.ShapeDtypeStruct(xs.shape, f32),
                   jax.ShapeDtypeStruct((1, CONV_HIST, d), f32),
                   jax.ShapeDtypeStruct((1, POOL_HIST, d), f32),
                   jax.ShapeDtypeStruct(zhist.shape, f32),
                   jax.ShapeDtypeStruct(phist.shape, f32)],
        scratch_shapes=[pltpu.VMEM(a.shape[1:], _BF16) for a in big]
                       + [pltpu.VMEM((1, Z_HALO + tile, d), _F32),
                          pltpu.VMEM((segs, Z_HALO + dec_rows, d), _F32)]
                       + [pltpu.VMEM((1, P_HALO + tile, d - k * group), _F32)
                          for k in range(len(POOL_WINDOWS))]
                       + [pltpu.VMEM((segs, P_HALO + dec_rows, d - k * group), _F32)
                          for k in range(len(POOL_WINDOWS))],
        compiler_params=_COMPILER_PARAMS,
        name=f"mixer_l{layer}",
    )(xp, xs, zhist, phist, *small, *big)


def _ffn_call(xp, xs, p, *, layer):
    seq, d = xp.shape
    tile = FFN_SUBTILES * PROMPT_SUB_ROWS
    n_tiles = seq // tile
    small = (p["g_ffn_pre"], p["g_ffn_post"])
    big = (p["w_up"], p["w_down"])
    prompt_spec = pl.BlockSpec((tile, d), lambda i: (jnp.minimum(i, n_tiles - 1), 0))
    return pl.pallas_call(
        functools.partial(_ffn_kernel, layer=layer, n_prompt_tiles=n_tiles),
        grid=(n_tiles + 1,),
        in_specs=[prompt_spec, _full_spec(xs.shape)]
                 + [_full_spec(a.shape) for a in small] + [_ANY_SPEC] * len(big),
        out_specs=[prompt_spec, _full_spec(xs.shape)],
        out_shape=[jax.ShapeDtypeStruct(xp.shape, xp.dtype),
                   jax.ShapeDtypeStruct(xs.shape, xs.dtype)],
        scratch_shapes=[pltpu.VMEM(a.shape[1:], _BF16) for a in big],
        compiler_params=_COMPILER_PARAMS,
        name=f"ffn_l{layer}",
    )(xp, xs, *small, *big)


def kernel(x_prompt, x_sample, cache_conv, cache_pool, w_in, b_gate, conv_w, w_conv_out,
           pool_w, pool_scale, w_pool_out, w_o, g_mix_pre, g_mix_post,
           w_up, w_down, g_ffn_pre, g_ffn_post):
    depth, d = g_mix_pre.shape
    batch, seq, _ = x_prompt.shape
    dec_batch, dec_seq, _ = x_sample.shape
    assert batch == 1
    assert seq % (PROMPT_SUB_ROWS * MIXER_SUBTILES) == 0
    assert seq % (PROMPT_SUB_ROWS * FFN_SUBTILES) == 0
    assert dec_seq >= POOL_HIST and dec_seq % SUBLANES == 0

    n_groups, group = pool_w.shape[1], pool_w.shape[2]
    params = dict(
        w_in=w_in, b_gate=b_gate, conv_w=conv_w, w_conv_out=w_conv_out,
        pool_w=pool_w.reshape(depth, n_groups * group, group), pool_scale=pool_scale,
        w_pool_out=w_pool_out, w_o=w_o, g_mix_pre=g_mix_pre, g_mix_post=g_mix_post,
        w_up=w_up, w_down=w_down, g_ffn_pre=g_ffn_pre, g_ffn_post=g_ffn_post)

    xp = x_prompt.reshape(seq, d)
    xs = x_sample.reshape(dec_batch * dec_seq, d)
    conv_p, pool_p, conv_s, pool_s = [], [], [], []
    for l in range(depth):
        xp, xs, zp, pp, zs, ps = _mixer_call(xp, xs, cache_conv[l], cache_pool[l], params,
                                             layer=l, dec_rows=dec_seq)
        xp, xs = _ffn_call(xp, xs, params, layer=l)
        conv_p.append(zp)
        pool_p.append(pp)
        conv_s.append(zs)
        pool_s.append(ps)
    return (xp.reshape(x_prompt.shape), xs.reshape(x_sample.shape),
            jnp.stack(conv_p), jnp.stack(pool_p), jnp.stack(conv_s), jnp.stack(pool_s))
```

```python
import functools

import jax
import jax.numpy as jnp
from jax import lax
from jax.experimental import pallas as pl
from jax.experimental.pallas import tpu as pltpu

EPS = 1e-6
CONV_W = 3
CONV_HIST = CONV_W - 1
POOL_WINDOWS = (2, 4, 8, 16)
POOL_HIST = max(POOL_WINDOWS) - 1
PROJ_B, PROJ_C, PROJ_V, PROJ_P, PROJ_GA, PROJ_GB = range(6)
PROJ_USE_ORDER = (PROJ_C, PROJ_V, PROJ_B, PROJ_P, PROJ_GA, PROJ_GB)
N_PROJ = len(PROJ_USE_ORDER)
PAST_LEN = 2048

SUBLANES = 8
Z_HALO = -(-CONV_HIST // SUBLANES) * SUBLANES
POOL_CARRY = -(-max(POOL_WINDOWS) // SUBLANES) * SUBLANES
P_HALO = POOL_CARRY + SUBLANES
assert all(win == 2 ** (k + 1) for k, win in enumerate(POOL_WINDOWS))
assert POOL_HIST <= POOL_CARRY

PROMPT_SUB_ROWS = 256
MIXER_SUBTILES = 2
FFN_SUBTILES = 4
WEIGHT_STAGE_BYTES = 1024 * 1024
WEIGHT_COPIES_IN_FLIGHT = 3
VMEM_LIMIT_BYTES = 56 * 1024 * 1024

_F32 = jnp.float32
_BF16 = jnp.bfloat16


def _rms_norm(x, g):
    ms = jnp.mean(x * x, axis=-1, keepdims=True)
    return x * lax.rsqrt(ms + EPS) * g


def _dot(a, b):
    return jnp.dot(a, b, preferred_element_type=_F32)


def _stage_rows(n_rows, n_cols):
    rows = n_rows
    while rows * n_cols * 4 > WEIGHT_STAGE_BYTES and rows % (2 * SUBLANES) == 0:
        rows //= 2
    return rows


class _WeightStream:
    def __init__(self, jobs, stages, sems):
        self.jobs = []
        used = [0] * len(stages)
        for src, dst, which in jobs:
            slot = used[which] % stages[which].shape[0]
            used[which] += 1
            copy = pltpu.make_async_copy(src, stages[which].at[slot], sems[which].at[slot])
            self.jobs.append((copy, stages[which], slot, dst))
        self.started = 0
        self.done = 0

    def require(self, n):
        n = min(n, len(self.jobs))
        while self.done < n:
            while self.started < min(self.done + 1 + WEIGHT_COPIES_IN_FLIGHT, len(self.jobs)):
                self.jobs[self.started][0].start()
                self.started += 1
            copy, stage, slot, dst = self.jobs[self.done]
            copy.wait()
            dst[...] = stage[slot].astype(_BF16)
            self.done += 1

    def advance(self, n):
        self.require(self.done + n)


class _Weights:
    def __init__(self, views, stream=None, ready=None, chunks_per_dot=0):
        self.views = views
        self.stream = stream
        self.ready = ready
        self.chunks_per_dot = chunks_per_dot

    def __getitem__(self, key):
        if self.stream is not None:
            self.stream.require(self.ready[key])
        ref, index = self.views[key]
        return ref[index]

    def dot(self, a, key):
        out = _dot(a, self[key])
        if self.stream is not None:
            self.stream.advance(self.chunks_per_dot)
        return out


def _run_with_weight_stream(blocks, views, aliases, n_dots, body):
    shapes, counts, specs, ready = [], [], [], {}
    for key, src, dst in blocks:
        n_rows, n_cols = src.shape
        rows = _stage_rows(n_rows, n_cols)
        if (rows, n_cols) not in shapes:
            shapes.append((rows, n_cols))
            counts.append(0)
        which = shapes.index((rows, n_cols))
        for k in range(n_rows // rows):
            chunk = pl.ds(k * rows, rows)
            specs.append((src.at[chunk], dst.at[chunk], which))
            counts[which] += 1
        ready[key] = len(specs)
    for key, carrier in aliases.items():
        ready[key] = ready[carrier]
    slots = [min(WEIGHT_COPIES_IN_FLIGHT + 1, count) for count in counts]

    def scoped(*scratch):
        stream = _WeightStream(specs, scratch[:len(shapes)], scratch[len(shapes):])
        body(_Weights(views, stream, ready, -(-len(specs) // n_dots)))
        stream.require(len(specs))

    pl.run_scoped(scoped,
                  *[pltpu.VMEM((n,) + shape, _F32) for n, shape in zip(slots, shapes)],
                  *[pltpu.SemaphoreType.DMA((n,)) for n in slots])


def _pool_window_sums(levels, s, row0, n):
    group = levels[0].shape[-1] // len(POOL_WINDOWS)
    acc = levels[0][s, row0:row0 + n, :]
    sums = []
    for k, win in enumerate(POOL_WINDOWS):
        shift = win // 2
        acc = acc + levels[k][s, row0 - shift:row0 - shift + n, :]
        sums.append(acc[:, :group])
        if k + 1 < len(POOL_WINDOWS):
            acc = acc[:, group:]
            levels[k + 1][s, row0:row0 + n, :] = acc
    return sums


def _mixer_tile(x_ref, y_ref, zstate_ref, pstate_ref, zbuf, levels, w, small, *,
                tile_index, segs, seg_rows, n_sub, start):
    d = x_ref.shape[-1]
    L = seg_rows
    M = segs * L
    T = n_sub * L
    group = d // len(POOL_WINDOWS)

    def front(r):
        hb = _rms_norm(x_ref[pl.ds(r * M, M), :], small["g_pre"]).astype(_BF16)
        z0, p0 = Z_HALO + r * L, P_HALO + r * L
        z = w.dot(hb, ("w_in", PROJ_C)) * w.dot(hb, ("w_in", PROJ_V))
        for s in range(segs):
            zbuf[s, z0:z0 + L, :] = z[s * L:(s + 1) * L]
        conv_w = small["conv_w"]
        ys = []
        for s in range(segs):
            y = conv_w[CONV_W - 1:CONV_W] * z[s * L:(s + 1) * L]
            for k in range(CONV_W - 1):
                off = z0 - (CONV_W - 1 - k)
                y = y + conv_w[k:k + 1] * zbuf[s, off:off + L, :]
            ys.append(y)
        y = ys[0] if segs == 1 else jnp.concatenate(ys, axis=0)
        ya = (w.dot(hb, ("w_in", PROJ_B)) * y).astype(_BF16)
        pp = w.dot(hb, ("w_in", PROJ_P))
        for s in range(segs):
            levels[0][s, p0:p0 + L, :] = pp[s * L:(s + 1) * L]
        sums = [_pool_window_sums(levels, s, p0, L) for s in range(segs)]
        pos = start + tile_index * T + r * L + lax.broadcasted_iota(jnp.int32, (L, 1), 0)
        dgs = []
        for gi, win in enumerate(POOL_WINDOWS):
            lanes = slice(gi * group, (gi + 1) * group)
            inv_cnt = 1.0 / jnp.minimum(pos + 1, win).astype(_F32)
            ds = [sums[s][gi] * inv_cnt - pp[s * L:(s + 1) * L, lanes] for s in range(segs)]
            dg = ds[0] if segs == 1 else jnp.concatenate(ds, axis=0)
            dgs.append(dg.astype(_BF16))
        b_gate = small["b_gate"]
        g_a = jax.nn.sigmoid(w.dot(hb, ("w_in", PROJ_GA)) + b_gate[:, :d])
        g_b = jax.nn.sigmoid(w.dot(hb, ("w_in", PROJ_GB)) + b_gate[:, d:])
        return ya, dgs, g_a, g_b

    def output(r, ya, dgs, g_a, g_b):
        branch_a = w.dot(ya, "w_conv_out")
        yb = jnp.concatenate([w.dot(dg, ("pool_w", gi)) for gi, dg in enumerate(dgs)], axis=-1)
        yb = (yb * small["pool_scale"]).astype(_BF16)
        branch_b = w.dot(yb, "w_pool_out")
        merged = (g_a * branch_a + g_b * branch_b).astype(_BF16)
        o = w.dot(merged, "w_o")
        rows = pl.ds(r * M, M)
        y_ref[rows, :] = x_ref[rows, :] + _rms_norm(o, small["g_post"])

    mixed = front(0)
    for r in range(n_sub):
        upcoming = front(r + 1) if r + 1 < n_sub else None
        output(r, *mixed)
        mixed = upcoming

    for s in range(segs):
        z_tail = zbuf[s, Z_HALO + T - CONV_HIST:Z_HALO + T, :]
        zstate_ref[s] = z_tail
        zbuf[s, Z_HALO - CONV_HIST:Z_HALO, :] = z_tail
        pstate_ref[s] = levels[0][s, P_HALO + T - POOL_HIST:P_HALO + T, :]
        for buf in levels:
            buf[s, P_HALO - POOL_CARRY:P_HALO, :] = buf[s, P_HALO + T - POOL_CARRY:P_HALO + T, :]


def _block(ref, row0, n_rows, col0, n_cols):
    return ref.at[pl.ds(row0, n_rows), pl.ds(col0, n_cols)]


def _mixer_kernel(xp_ref, xs_ref, zhist_ref, phist_ref, b_gate_ref, conv_w_ref,
                  pool_scale_ref, g_pre_ref, g_post_ref,
                  w_in_hbm, w_conv_out_hbm, pool_w_hbm, w_pool_out_hbm, w_o_hbm,
                  yp_ref, ys_ref, zstate_p_ref, pstate_p_ref, zstate_s_ref, pstate_s_ref,
                  w_in_v, w_conv_out_v, pool_w_v, w_pool_out_v, w_o_v,
                  zbuf_p, zbuf_s, *level_bufs, layer, n_prompt_tiles, dec_rows):
    i = pl.program_id(0)
    d = xp_ref.shape[-1]
    n_groups = len(POOL_WINDOWS)
    group = pool_w_v.shape[-1]
    levels_p = level_bufs[:n_groups]
    levels_s = level_bufs[n_groups:]

    square = (0, d, 0, d)
    layout = [(("w_in", k), w_in_v, w_in_hbm, (0, d, k * d, d)) for k in PROJ_USE_ORDER]
    layout.append(("w_conv_out", w_conv_out_v, w_conv_out_hbm, square))
    layout.append((("pool_w", 0), pool_w_v, pool_w_hbm, (0, n_groups * group, 0, group)))
    layout.append(("w_pool_out", w_pool_out_v, w_pool_out_hbm, square))
    layout.append(("w_o", w_o_v, w_o_hbm, square))
    blocks = [(key, _block(hbm.at[layer], *blk), _block(vmem, *blk))
              for key, vmem, hbm, blk in layout]
    views = {key: (vmem, (slice(blk[0], blk[0] + blk[1]), slice(blk[2], blk[2] + blk[3])))
             for key, vmem, _, blk in layout}
    for g in range(n_groups):
        views[("pool_w", g)] = (pool_w_v, (slice(g * group, (g + 1) * group), slice(None)))
    ready_alias = {("pool_w", g): ("pool_w", 0) for g in range(1, n_groups)}

    small = dict(
        b_gate=b_gate_ref[layer:layer + 1, :], conv_w=conv_w_ref[layer],
        pool_scale=pool_scale_ref[layer:layer + 1, :],
        g_pre=g_pre_ref[layer:layer + 1, :], g_post=g_post_ref[layer:layer + 1, :])

    def zero_rows(buf, n):
        buf[:, 0:n, :] = jnp.zeros((buf.shape[0], n, buf.shape[2]), _F32)

    def prompt_tile(w, tile_index):
        _mixer_tile(xp_ref, yp_ref, zstate_p_ref, pstate_p_ref, zbuf_p, levels_p, w, small,
                    tile_index=tile_index, segs=1, seg_rows=PROMPT_SUB_ROWS,
                    n_sub=MIXER_SUBTILES, start=0)

    @pl.when(i == 0)
    def _first_tile():
        zero_rows(zbuf_p, Z_HALO)
        for buf in levels_p:
            zero_rows(buf, P_HALO)
        n_dots = MIXER_SUBTILES * (N_PROJ + n_groups + 3)
        _run_with_weight_stream(blocks, views, ready_alias, n_dots,
                                lambda w: prompt_tile(w, 0))

    @pl.when(jnp.logical_and(i > 0, i < n_prompt_tiles))
    def _prompt_tile():
        prompt_tile(_Weights(views), i)

    @pl.when(i == n_prompt_tiles)
    def _decode_rows():
        segs = zhist_ref.shape[0]
        zbuf_s[:, Z_HALO - CONV_HIST:Z_HALO, :] = zhist_ref[...]
        for buf in levels_s:
            zero_rows(buf, P_HALO)
        levels_s[0][:, P_HALO - POOL_HIST:P_HALO, :] = phist_ref[...]
        for s in range(segs):
            _pool_window_sums(levels_s, s, P_HALO - POOL_CARRY, POOL_CARRY)
        _mixer_tile(xs_ref, ys_ref, zstate_s_ref, pstate_s_ref, zbuf_s, levels_s,
                    _Weights(views), small, tile_index=0, segs=segs, seg_rows=dec_rows,
                    n_sub=1, start=PAST_LEN)


def _ffn_rows(x_ref, y_ref, w, g_pre, g_post, *, sub_rows, n_chunks):
    n_sub = x_ref.shape[0] // sub_rows
    hbs = [_rms_norm(x_ref[pl.ds(r * sub_rows, sub_rows), :], g_pre).astype(_BF16)
           for r in range(n_sub)]
    fs = [None] * n_sub
    for k in range(n_chunks):
        for r in range(n_sub):
            u = w.dot(hbs[r], ("w_up", k))
            a = jnp.square(jnp.maximum(u, 0.0)).astype(_BF16)
            part = w.dot(a, ("w_down", k))
            fs[r] = part if fs[r] is None else fs[r] + part
    for r in range(n_sub):
        rows = pl.ds(r * sub_rows, sub_rows)
        y_ref[rows, :] = x_ref[rows, :] + _rms_norm(fs[r], g_post)


def _ffn_kernel(xp_ref, xs_ref, g_pre_ref, g_post_ref, w_up_hbm, w_down_hbm,
                yp_ref, ys_ref, w_up_v, w_down_v, *, layer, n_prompt_tiles):
    i = pl.program_id(0)
    d, d_ff = w_up_v.shape
    n_chunks = d_ff // d

    layout = []
    for k in range(n_chunks):
        layout.append((("w_up", k), w_up_v, w_up_hbm, (0, d, k * d, d)))
        layout.append((("w_down", k), w_down_v, w_down_hbm, (k * d, d, 0, d)))
    blocks = [(key, _block(hbm.at[layer], *blk), _block(vmem, *blk))
              for key, vmem, hbm, blk in layout]
    views = {key: (vmem, (slice(blk[0], blk[0] + blk[1]), slice(blk[2], blk[2] + blk[3])))
             for key, vmem, _, blk in layout}

    g_pre = g_pre_ref[layer:layer + 1, :]
    g_post = g_post_ref[layer:layer + 1, :]

    def prompt_tile(w):
        _ffn_rows(xp_ref, yp_ref, w, g_pre, g_post, sub_rows=PROMPT_SUB_ROWS, n_chunks=n_chunks)

    @pl.when(i == 0)
    def _first_tile():
        _run_with_weight_stream(blocks, views, {}, 2 * n_chunks * FFN_SUBTILES, prompt_tile)

    @pl.when(jnp.logical_and(i > 0, i < n_prompt_tiles))
    def _prompt_tile():
        prompt_tile(_Weights(views))

    @pl.when(i == n_prompt_tiles)
    def _decode_rows():
        _ffn_rows(xs_ref, ys_ref, _Weights(views), g_pre, g_post,
                  sub_rows=xs_ref.shape[0], n_chunks=n_chunks)


def _full_spec(shape):
    zeros = (0,) * len(shape)
    return pl.BlockSpec(shape, lambda i: zeros)


_ANY_SPEC = pl.BlockSpec(memory_space=pl.ANY)
_COMPILER_PARAMS = pltpu.CompilerParams(
    dimension_semantics=("arbitrary",), vmem_limit_bytes=VMEM_LIMIT_BYTES)


def _mixer_call(xp, xs, zhist, phist, p, *, layer, dec_rows):
    seq, d = xp.shape
    tile = MIXER_SUBTILES * PROMPT_SUB_ROWS
    n_tiles = seq // tile
    segs = zhist.shape[0]
    group = d // len(POOL_WINDOWS)
    small = (p["b_gate"], p["conv_w"], p["pool_scale"], p["g_mix_pre"], p["g_mix_post"])
    big = (p["w_in"], p["w_conv_out"], p["pool_w"], p["w_pool_out"], p["w_o"])
    prompt_spec = pl.BlockSpec((tile, d), lambda i: (jnp.minimum(i, n_tiles - 1), 0))
    f32 = xp.dtype
    kern = functools.partial(_mixer_kernel, layer=layer, n_prompt_tiles=n_tiles,
                             dec_rows=dec_rows)
    return pl.pallas_call(
        kern,
        grid=(n_tiles + 1,),
        in_specs=[prompt_spec, _full_spec(xs.shape), _full_spec(zhist.shape),
                  _full_spec(phist.shape)]
                 + [_full_spec(a.shape) for a in small] + [_ANY_SPEC] * len(big),
        out_specs=[prompt_spec, _full_spec(xs.shape),
                   _full_spec((1, CONV_HIST, d)), _full_spec((1, POOL_HIST, d)),
                   _full_spec(zhist.shape), _full_spec(phist.shape)],
        out_shape=[jax.ShapeDtypeStruct(xp.shape, f32), jax.ShapeDtypeStruct(xs.shape, f32),
                   jax.ShapeDtypeStruct((1, CONV_HIST, d), f32),
                   jax.ShapeDtypeStruct((1, POOL_HIST, d), f32),
                   jax.ShapeDtypeStruct(zhist.shape, f32),
                   jax.ShapeDtypeStruct(phist.shape, f32)],
        scratch_shapes=[pltpu.VMEM(a.shape[1:], _BF16) for a in big]
                       + [pltpu.VMEM((1, Z_HALO + tile, d), _F32),
                          pltpu.VMEM((segs, Z_HALO + dec_rows, d), _F32)]
                       + [pltpu.VMEM((1, P_HALO + tile, d - k * group), _F32)
                          for k in range(len(POOL_WINDOWS))]
                       + [pltpu.VMEM((segs, P_HALO + dec_rows, d - k * group), _F32)
                          for k in range(len(POOL_WINDOWS))],
        compiler_params=_COMPILER_PARAMS,
        name=f"mixer_l{layer}",
    )(xp, xs, zhist, phist, *small, *big)


def _ffn_call(xp, xs, p, *, layer):
    seq, d = xp.shape
    tile = FFN_SUBTILES * PROMPT_SUB_ROWS
    n_tiles = seq // tile
    small = (p["g_ffn_pre"], p["g_ffn_post"])
    big = (p["w_up"], p["w_down"])
    prompt_spec = pl.BlockSpec((tile, d), lambda i: (jnp.minimum(i, n_tiles - 1), 0))
    return pl.pallas_call(
        functools.partial(_ffn_kernel, layer=layer, n_prompt_tiles=n_tiles),
        grid=(n_tiles + 1,),
        in_specs=[prompt_spec, _full_spec(xs.shape)]
                 + [_full_spec(a.shape) for a in small] + [_ANY_SPEC] * len(big),
        out_specs=[prompt_spec, _full_spec(xs.shape)],
        out_shape=[jax.ShapeDtypeStruct(xp.shape, xp.dtype),
                   jax.ShapeDtypeStruct(xs.shape, xs.dtype)],
        scratch_shapes=[pltpu.VMEM(a.shape[1:], _BF16) for a in big],
        compiler_params=_COMPILER_PARAMS,
        name=f"ffn_l{layer}",
    )(xp, xs, *small, *big)


def kernel(x_prompt, x_sample, cache_conv, cache_pool, w_in, b_gate, conv_w, w_conv_out,
           pool_w, pool_scale, w_pool_out, w_o, g_mix_pre, g_mix_post,
           w_up, w_down, g_ffn_pre, g_ffn_post):
    depth, d = g_mix_pre.shape
    batch, seq, _ = x_prompt.shape
    dec_batch, dec_seq, _ = x_sample.shape
    assert batch == 1
    assert seq % (PROMPT_SUB_ROWS * MIXER_SUBTILES) == 0
    assert seq % (PROMPT_SUB_ROWS * FFN_SUBTILES) == 0
    assert dec_seq >= POOL_HIST and dec_seq % SUBLANES == 0

    n_groups, group = pool_w.shape[1], pool_w.shape[2]
    params = dict(
        w_in=w_in, b_gate=b_gate, conv_w=conv_w, w_conv_out=w_conv_out,
        pool_w=pool_w.reshape(depth, n_groups * group, group), pool_scale=pool_scale,
        w_pool_out=w_pool_out, w_o=w_o, g_mix_pre=g_mix_pre, g_mix_post=g_mix_post,
        w_up=w_up, w_down=w_down, g_ffn_pre=g_ffn_pre, g_ffn_post=g_ffn_post)

    xp = x_prompt.reshape(seq, d)
    xs = x_sample.reshape(dec_batch * dec_seq, d)
    conv_p, pool_p, conv_s, pool_s = [], [], [], []
    for l in range(depth):
        xp, xs, zp, pp, zs, ps = _mixer_call(xp, xs, cache_conv[l], cache_pool[l], params,
                                             layer=l, dec_rows=dec_seq)
        xp, xs = _ffn_call(xp, xs, params, layer=l)
        conv_p.append(zp)
        pool_p.append(pp)
        conv_s.append(zs)
        pool_s.append(ps)
    return (xp.reshape(x_prompt.shape), xs.reshape(x_sample.shape),
            jnp.stack(conv_p), jnp.stack(pool_p), jnp.stack(conv_s), jnp.stack(pool_s))
```

```python
import functools

import jax
import jax.numpy as jnp
from jax import lax
from jax.experimental import pallas as pl
from jax.experimental.pallas import tpu as pltpu

EPS = 1e-6
CONV_W = 3
CONV_HIST = CONV_W - 1
POOL_WINDOWS = (2, 4, 8, 16)
POOL_HIST = max(POOL_WINDOWS) - 1
N_PROJ = 6
PAST_LEN = 2048

SUBLANES = 8
Z_HALO = -(-CONV_HIST // SUBLANES) * SUBLANES
POOL_CARRY = -(-max(POOL_WINDOWS) // SUBLANES) * SUBLANES
P_HALO = POOL_CARRY + SUBLANES
assert all(win == 2 ** (k + 1) for k, win in enumerate(POOL_WINDOWS))
assert POOL_HIST <= POOL_CARRY

PROMPT_SUB_ROWS = 256
MIXER_SUBTILES = 4
FFN_SUBTILES = 4
WEIGHT_STAGE_BYTES = 1024 * 1024
WEIGHT_COPIES_IN_FLIGHT = 1
VMEM_LIMIT_BYTES = 56 * 1024 * 1024

_F32 = jnp.float32
_BF16 = jnp.bfloat16


def _rms_norm(x, g):
    ms = jnp.mean(x * x, axis=-1, keepdims=True)
    return x * lax.rsqrt(ms + EPS) * g


def _dot(a, b):
    return jnp.dot(a, b, preferred_element_type=_F32)


def _stage_rows(n_rows, n_cols):
    rows = n_rows
    while rows * n_cols * 4 > WEIGHT_STAGE_BYTES and rows % (2 * SUBLANES) == 0:
        rows //= 2
    return rows


def _load_weights_as_bf16(pairs):
    plan = []
    for src, _ in pairs:
        shape = (_stage_rows(*src.shape), src.shape[1])
        n_chunks = src.shape[0] // shape[0]
        for k, (other, count) in enumerate(plan):
            if other == shape:
                plan[k] = (shape, count + n_chunks)
                break
        else:
            plan.append((shape, n_chunks))
    shapes = [shape for shape, _ in plan]
    slots = [min(WEIGHT_COPIES_IN_FLIGHT + 1, count) for _, count in plan]

    def body(*scratch):
        stages, sems = scratch[:len(shapes)], scratch[len(shapes):]
        used = [0] * len(shapes)
        jobs = []
        for src, dst in pairs:
            rows = _stage_rows(*src.shape)
            which = shapes.index((rows, src.shape[1]))
            for k in range(src.shape[0] // rows):
                slot = used[which] % slots[which]
                used[which] += 1
                chunk = pl.ds(k * rows, rows)
                copy = pltpu.make_async_copy(
                    src.at[chunk], stages[which].at[slot], sems[which].at[slot])
                jobs.append((copy, stages[which], slot, dst, chunk))
        for copy, *_ in jobs[:WEIGHT_COPIES_IN_FLIGHT]:
            copy.start()
        for j, (copy, stage, slot, dst, chunk) in enumerate(jobs):
            if j + WEIGHT_COPIES_IN_FLIGHT < len(jobs):
                jobs[j + WEIGHT_COPIES_IN_FLIGHT][0].start()
            copy.wait()
            dst[chunk, :] = stage[slot].astype(_BF16)

    pl.run_scoped(body,
                  *[pltpu.VMEM((n,) + shape, _F32) for n, shape in zip(slots, shapes)],
                  *[pltpu.SemaphoreType.DMA((n,)) for n in slots])


def _pool_window_sums(levels, s, row0, n):
    group = levels[0].shape[-1] // len(POOL_WINDOWS)
    acc = levels[0][s, row0:row0 + n, :]
    sums = []
    for k, win in enumerate(POOL_WINDOWS):
        shift = win // 2
        acc = acc + levels[k][s, row0 - shift:row0 - shift + n, :]
        sums.append(acc[:, :group])
        if k + 1 < len(POOL_WINDOWS):
            acc = acc[:, group:]
            levels[k + 1][s, row0:row0 + n, :] = acc
    return sums


def _mixer_tile(x_ref, y_ref, zstate_ref, pstate_ref, zbuf, levels, w, *,
                tile_index, segs, seg_rows, n_sub, start):
    d = x_ref.shape[-1]
    L = seg_rows
    M = segs * L
    group = d // len(POOL_WINDOWS)

    def project(r):
        hb = _rms_norm(x_ref[pl.ds(r * M, M), :], w["g_pre"]).astype(_BF16)
        return [_dot(hb, w["w_in"][:, k * d:(k + 1) * d]) for k in range(N_PROJ)]

    def mix(r, projections):
        pb, pc, pv, pp, pga, pgb = projections
        z = pc * pv
        for s in range(segs):
            zbuf[s, Z_HALO:Z_HALO + L, :] = z[s * L:(s + 1) * L]
        conv_w = w["conv_w"]
        ys = []
        for s in range(segs):
            y = conv_w[CONV_W - 1:CONV_W] * z[s * L:(s + 1) * L]
            for k in range(CONV_W - 1):
                off = Z_HALO - (CONV_W - 1 - k)
                y = y + conv_w[k:k + 1] * zbuf[s, off:off + L, :]
            ys.append(y)
        y = ys[0] if segs == 1 else jnp.concatenate(ys, axis=0)
        ya = (pb * y).astype(_BF16)
        for s in range(segs):
            levels[0][s, P_HALO:P_HALO + L, :] = pp[s * L:(s + 1) * L]
        sums = [_pool_window_sums(levels, s, P_HALO, L) for s in range(segs)]
        pos = start + (tile_index * n_sub + r) * L + lax.broadcasted_iota(jnp.int32, (L, 1), 0)
        dgs = []
        for gi, win in enumerate(POOL_WINDOWS):
            lanes = slice(gi * group, (gi + 1) * group)
            inv_cnt = 1.0 / jnp.minimum(pos + 1, win).astype(_F32)
            ds = [sums[s][gi] * inv_cnt - pp[s * L:(s + 1) * L, lanes] for s in range(segs)]
            dg = ds[0] if segs == 1 else jnp.concatenate(ds, axis=0)
            dgs.append(dg.astype(_BF16))
        for s in range(segs):
            zbuf[s, Z_HALO - CONV_HIST:Z_HALO, :] = zbuf[s, Z_HALO + L - CONV_HIST:Z_HALO + L, :]
            for buf in levels:
                buf[s, P_HALO - POOL_CARRY:P_HALO, :] = buf[s, P_HALO + L - POOL_CARRY:P_HALO + L, :]
        b_gate = w["b_gate"]
        g_a = jax.nn.sigmoid(pga + b_gate[:, :d])
        g_b = jax.nn.sigmoid(pgb + b_gate[:, d:])
        return ya, dgs, g_a, g_b

    def output(r, ya, dgs, g_a, g_b):
        branch_a = _dot(ya, w["w_conv_out"][...])
        yb = jnp.concatenate([_dot(dg, w["pool_w"][gi][...]) for gi, dg in enumerate(dgs)],
                             axis=-1)
        yb = (yb * w["pool_scale"]).astype(_BF16)
        branch_b = _dot(yb, w["w_pool_out"][...])
        merged = (g_a * branch_a + g_b * branch_b).astype(_BF16)
        o = _dot(merged, w["w_o"][...])
        rows = pl.ds(r * M, M)
        y_ref[rows, :] = x_ref[rows, :] + _rms_norm(o, w["g_post"])

    projections = project(0)
    for r in range(n_sub):
        upcoming = project(r + 1) if r + 1 < n_sub else None
        output(r, *mix(r, projections))
        projections = upcoming

    for s in range(segs):
        zstate_ref[s] = zbuf[s, Z_HALO - CONV_HIST:Z_HALO, :]
        pstate_ref[s] = levels[0][s, P_HALO - POOL_HIST:P_HALO, :]


def _mixer_kernel(xp_ref, xs_ref, zhist_ref, phist_ref, b_gate_ref, conv_w_ref,
                  pool_scale_ref, g_pre_ref, g_post_ref,
                  w_in_hbm, w_conv_out_hbm, pool_w_hbm, w_pool_out_hbm, w_o_hbm,
                  yp_ref, ys_ref, zstate_p_ref, pstate_p_ref, zstate_s_ref, pstate_s_ref,
                  w_in_v, w_conv_out_v, pool_w_v, w_pool_out_v, w_o_v,
                  zbuf_p, zbuf_s, *level_bufs, layer, n_prompt_tiles, dec_rows):
    i = pl.program_id(0)
    group = pool_w_v.shape[-1]
    levels_p = level_bufs[:len(POOL_WINDOWS)]
    levels_s = level_bufs[len(POOL_WINDOWS):]

    def zero_rows(buf, n):
        buf[:, 0:n, :] = jnp.zeros((buf.shape[0], n, buf.shape[2]), _F32)

    @pl.when(i == 0)
    def _first_step():
        _load_weights_as_bf16([
            (w_in_hbm.at[layer], w_in_v),
            (w_conv_out_hbm.at[layer], w_conv_out_v),
            (pool_w_hbm.at[layer], pool_w_v),
            (w_pool_out_hbm.at[layer], w_pool_out_v),
            (w_o_hbm.at[layer], w_o_v)])
        zero_rows(zbuf_p, Z_HALO)
        for buf in levels_p:
            zero_rows(buf, P_HALO)

    w = dict(
        w_in=w_in_v, w_conv_out=w_conv_out_v, w_pool_out=w_pool_out_v, w_o=w_o_v,
        pool_w=[pool_w_v.at[pl.ds(g * group, group)] for g in range(len(POOL_WINDOWS))],
        b_gate=b_gate_ref[layer:layer + 1, :], conv_w=conv_w_ref[layer],
        pool_scale=pool_scale_ref[layer:layer + 1, :],
        g_pre=g_pre_ref[layer:layer + 1, :], g_post=g_post_ref[layer:layer + 1, :])

    @pl.when(i < n_prompt_tiles)
    def _prompt_tile():
        _mixer_tile(xp_ref, yp_ref, zstate_p_ref, pstate_p_ref, zbuf_p, levels_p, w,
                    tile_index=i, segs=1, seg_rows=PROMPT_SUB_ROWS, n_sub=MIXER_SUBTILES,
                    start=0)

    @pl.when(i == n_prompt_tiles)
    def _decode_rows():
        segs = zhist_ref.shape[0]
        zbuf_s[:, Z_HALO - CONV_HIST:Z_HALO, :] = zhist_ref[...]
        for buf in levels_s:
            zero_rows(buf, P_HALO)
        levels_s[0][:, P_HALO - POOL_HIST:P_HALO, :] = phist_ref[...]
        for s in range(segs):
            _pool_window_sums(levels_s, s, P_HALO - POOL_CARRY, POOL_CARRY)
        _mixer_tile(xs_ref, ys_ref, zstate_s_ref, pstate_s_ref, zbuf_s, levels_s, w,
                    tile_index=0, segs=segs, seg_rows=dec_rows, n_sub=1, start=PAST_LEN)


def _ffn_rows(x_ref, y_ref, w_up, w_down, g_pre, g_post, *, sub_rows, chunk):
    n_sub = x_ref.shape[0] // sub_rows
    n_chunks = w_up.shape[-1] // chunk
    hbs = [_rms_norm(x_ref[pl.ds(r * sub_rows, sub_rows), :], g_pre).astype(_BF16)
           for r in range(n_sub)]
    fs = [None] * n_sub
    for k in range(n_chunks):
        for r in range(n_sub):
            u = _dot(hbs[r], w_up[:, k * chunk:(k + 1) * chunk])
            a = jnp.square(jnp.maximum(u, 0.0)).astype(_BF16)
            part = _dot(a, w_down[k * chunk:(k + 1) * chunk, :])
            fs[r] = part if fs[r] is None else fs[r] + part
    for r in range(n_sub):
        rows = pl.ds(r * sub_rows, sub_rows)
        y_ref[rows, :] = x_ref[rows, :] + _rms_norm(fs[r], g_post)


def _ffn_kernel(xp_ref, xs_ref, g_pre_ref, g_post_ref, w_up_hbm, w_down_hbm,
                yp_ref, ys_ref, w_up_v, w_down_v, *, layer, n_prompt_tiles):
    i = pl.program_id(0)
    d = xp_ref.shape[-1]

    @pl.when(i == 0)
    def _first_step():
        _load_weights_as_bf16([(w_up_hbm.at[layer], w_up_v), (w_down_hbm.at[layer], w_down_v)])

    g_pre = g_pre_ref[layer:layer + 1, :]
    g_post = g_post_ref[layer:layer + 1, :]

    @pl.when(i < n_prompt_tiles)
    def _prompt_tile():
        _ffn_rows(xp_ref, yp_ref, w_up_v, w_down_v, g_pre, g_post,
                  sub_rows=PROMPT_SUB_ROWS, chunk=d)

    @pl.when(i == n_prompt_tiles)
    def _decode_rows():
        _ffn_rows(xs_ref, ys_ref, w_up_v, w_down_v, g_pre, g_post,
                  sub_rows=xs_ref.shape[0], chunk=d)


def _full_spec(shape):
    zeros = (0,) * len(shape)
    return pl.BlockSpec(shape, lambda i: zeros)


_ANY_SPEC = pl.BlockSpec(memory_space=pl.ANY)
_COMPILER_PARAMS = pltpu.CompilerParams(
    dimension_semantics=("arbitrary",), vmem_limit_bytes=VMEM_LIMIT_BYTES)


def _carry_buffers(segs, rows, d):
    group = d // len(POOL_WINDOWS)
    return ([pltpu.VMEM((segs, Z_HALO + rows, d), _F32)],
            [pltpu.VMEM((segs, P_HALO + rows, d - k * group), _F32)
             for k in range(len(POOL_WINDOWS))])


def _mixer_call(xp, xs, zhist, phist, p, *, layer, dec_rows):
    seq, d = xp.shape
    tile = MIXER_SUBTILES * PROMPT_SUB_ROWS
    n_tiles = seq // tile
    segs = zhist.shape[0]
    small = (p["b_gate"], p["conv_w"], p["pool_scale"], p["g_mix_pre"], p["g_mix_post"])
    big = (p["w_in"], p["w_conv_out"], p["pool_w"], p["w_pool_out"], p["w_o"])
    prompt_spec = pl.BlockSpec((tile, d), lambda i: (jnp.minimum(i, n_tiles - 1), 0))
    f32 = xp.dtype
    zbuf_p, levels_p = _carry_buffers(1, PROMPT_SUB_ROWS, d)
    zbuf_s, levels_s = _carry_buffers(segs, dec_rows, d)
    kern = functools.partial(_mixer_kernel, layer=layer, n_prompt_tiles=n_tiles,
                             dec_rows=dec_rows)
    return pl.pallas_call(
        kern,
        grid=(n_tiles + 1,),
        in_specs=[prompt_spec, _full_spec(xs.shape), _full_spec(zhist.shape),
                  _full_spec(phist.shape)]
                 + [_full_spec(a.shape) for a in small] + [_ANY_SPEC] * len(big),
        out_specs=[prompt_spec, _full_spec(xs.shape),
                   _full_spec((1, CONV_HIST, d)), _full_spec((1, POOL_HIST, d)),
                   _full_spec(zhist.shape), _full_spec(phist.shape)],
        out_shape=[jax.ShapeDtypeStruct(xp.shape, f32), jax.ShapeDtypeStruct(xs.shape, f32),
                   jax.ShapeDtypeStruct((1, CONV_HIST, d), f32),
                   jax.ShapeDtypeStruct((1, POOL_HIST, d), f32),
                   jax.ShapeDtypeStruct(zhist.shape, f32),
                   jax.ShapeDtypeStruct(phist.shape, f32)],
        scratch_shapes=[pltpu.VMEM(a.shape[1:], _BF16) for a in big]
                       + zbuf_p + zbuf_s + levels_p + levels_s,
        compiler_params=_COMPILER_PARAMS,
        name=f"mixer_l{layer}",
    )(xp, xs, zhist, phist, *small, *big)


def _ffn_call(xp, xs, p, *, layer):
    seq, d = xp.shape
    tile = FFN_SUBTILES * PROMPT_SUB_ROWS
    n_tiles = seq // tile
    small = (p["g_ffn_pre"], p["g_ffn_post"])
    big = (p["w_up"], p["w_down"])
    prompt_spec = pl.BlockSpec((tile, d), lambda i: (jnp.minimum(i, n_tiles - 1), 0))
    return pl.pallas_call(
        functools.partial(_ffn_kernel, layer=layer, n_prompt_tiles=n_tiles),
        grid=(n_tiles + 1,),
        in_specs=[prompt_spec, _full_spec(xs.shape)]
                 + [_full_spec(a.shape) for a in small] + [_ANY_SPEC] * len(big),
        out_specs=[prompt_spec, _full_spec(xs.shape)],
        out_shape=[jax.ShapeDtypeStruct(xp.shape, xp.dtype),
                   jax.ShapeDtypeStruct(xs.shape, xs.dtype)],
        scratch_shapes=[pltpu.VMEM(a.shape[1:], _BF16) for a in big],
        compiler_params=_COMPILER_PARAMS,
        name=f"ffn_l{layer}",
    )(xp, xs, *small, *big)


def kernel(x_prompt, x_sample, cache_conv, cache_pool, w_in, b_gate, conv_w, w_conv_out,
           pool_w, pool_scale, w_pool_out, w_o, g_mix_pre, g_mix_post,
           w_up, w_down, g_ffn_pre, g_ffn_post):
    depth, d = g_mix_pre.shape
    batch, seq, _ = x_prompt.shape
    dec_batch, dec_seq, _ = x_sample.shape
    assert batch == 1
    assert seq % (PROMPT_SUB_ROWS * MIXER_SUBTILES) == 0
    assert seq % (PROMPT_SUB_ROWS * FFN_SUBTILES) == 0
    assert dec_seq >= POOL_HIST and dec_seq % SUBLANES == 0

    n_groups, group = pool_w.shape[1], pool_w.shape[2]
    params = dict(
        w_in=w_in, b_gate=b_gate, conv_w=conv_w, w_conv_out=w_conv_out,
        pool_w=pool_w.reshape(depth, n_groups * group, group), pool_scale=pool_scale,
        w_pool_out=w_pool_out, w_o=w_o, g_mix_pre=g_mix_pre, g_mix_post=g_mix_post,
        w_up=w_up, w_down=w_down, g_ffn_pre=g_ffn_pre, g_ffn_post=g_ffn_post)

    xp = x_prompt.reshape(seq, d)
    xs = x_sample.reshape(dec_batch * dec_seq, d)
    conv_p, pool_p, conv_s, pool_s = [], [], [], []
    for l in range(depth):
        xp, xs, zp, pp, zs, ps = _mixer_call(xp, xs, cache_conv[l], cache_pool[l], params,
                                             layer=l, dec_rows=dec_seq)
        xp, xs = _ffn_call(xp, xs, params, layer=l)
        conv_p.append(zp)
        pool_p.append(pp)
        conv_s.append(zs)
        pool_s.append(ps)
    return (xp.reshape(x_prompt.shape), xs.reshape(x_sample.shape),
            jnp.stack(conv_p), jnp.stack(pool_p), jnp.stack(conv_s), jnp.stack(pool_s))
```

```python
import functools

import jax
import jax.numpy as jnp
from jax import lax
from jax.experimental import pallas as pl
from jax.experimental.pallas import tpu as pltpu

EPS = 1e-6
CONV_W = 3
CONV_HIST = CONV_W - 1
POOL_WINDOWS = (2, 4, 8, 16)
POOL_HIST = max(POOL_WINDOWS) - 1
N_PROJ = 6
PAST_LEN = 2048

SUBLANES = 8
Z_HALO = -(-CONV_HIST // SUBLANES) * SUBLANES
POOL_CARRY = -(-max(POOL_WINDOWS) // SUBLANES) * SUBLANES
P_HALO = POOL_CARRY + SUBLANES
assert all(win == 2 ** (k + 1) for k, win in enumerate(POOL_WINDOWS))
assert POOL_HIST <= POOL_CARRY

PROMPT_SUB_ROWS = 256
MIXER_SUBTILES = 2
FFN_SUBTILES = 4
WEIGHT_STAGE_BYTES = 1024 * 1024
WEIGHT_COPIES_IN_FLIGHT = 7
BF16_SUBLANES = 2 * SUBLANES
ROUND_SLAB_ELEMS = 16 * 1024
VMEM_LIMIT_BYTES = 56 * 1024 * 1024

_F32 = jnp.float32
_BF16 = jnp.bfloat16


def _rms_norm(x, g):
    ms = jnp.mean(x * x, axis=-1, keepdims=True)
    return x * lax.rsqrt(ms + EPS) * g


def _dot(a, b):
    return jnp.dot(a, b, preferred_element_type=_F32)


def _stage_rows(n_rows, n_cols):
    rows = n_rows
    while rows * n_cols * 4 > WEIGHT_STAGE_BYTES and rows % (2 * SUBLANES) == 0:
        rows //= 2
    return rows


def _round_rows(src, dst):
    n_rows, n_cols = src.shape
    slab = BF16_SUBLANES
    while slab * n_cols < ROUND_SLAB_ELEMS and n_rows % (2 * slab) == 0:
        slab *= 2

    def round_slab(t, carry):
        rows = pl.ds(pl.multiple_of(t * slab, slab), slab)
        dst[rows, :] = src[rows, :].astype(_BF16)
        return carry

    lax.fori_loop(0, n_rows // slab, round_slab, 0)


def _column_blocks(src, dst, width):
    n_rows, n_cols = dst.shape
    return [(src.at[pl.ds(0, n_rows), pl.ds(c, width)], dst.at[pl.ds(0, n_rows), pl.ds(c, width)])
            for c in range(0, n_cols, width)]


def _load_weights_as_bf16(pairs):
    plan = []
    for src, _ in pairs:
        shape = (_stage_rows(*src.shape), src.shape[1])
        n_chunks = src.shape[0] // shape[0]
        for k, (other, count) in enumerate(plan):
            if other == shape:
                plan[k] = (shape, count + n_chunks)
                break
        else:
            plan.append((shape, n_chunks))
    shapes = [shape for shape, _ in plan]
    slots = [min(WEIGHT_COPIES_IN_FLIGHT + 1, count) for _, count in plan]

    def body(*scratch):
        stages, sems = scratch[:len(shapes)], scratch[len(shapes):]
        used = [0] * len(shapes)
        jobs = []
        for src, dst in pairs:
            rows = _stage_rows(*src.shape)
            which = shapes.index((rows, src.shape[1]))
            for k in range(src.shape[0] // rows):
                slot = used[which] % slots[which]
                used[which] += 1
                chunk = pl.ds(k * rows, rows)
                copy = pltpu.make_async_copy(
                    src.at[chunk], stages[which].at[slot], sems[which].at[slot])
                jobs.append((copy, stages[which], slot, dst, chunk))
        for copy, *_ in jobs[:WEIGHT_COPIES_IN_FLIGHT]:
            copy.start()
        for j, (copy, stage, slot, dst, chunk) in enumerate(jobs):
            if j + WEIGHT_COPIES_IN_FLIGHT < len(jobs):
                jobs[j + WEIGHT_COPIES_IN_FLIGHT][0].start()
            copy.wait()
            _round_rows(stage.at[slot], dst.at[chunk])

    pl.run_scoped(body,
                  *[pltpu.VMEM((n,) + shape, _F32) for n, shape in zip(slots, shapes)],
                  *[pltpu.SemaphoreType.DMA((n,)) for n in slots])


def _pool_window_sums(levels, s, row0, n):
    group = levels[0].shape[-1] // len(POOL_WINDOWS)
    acc = levels[0][s, row0:row0 + n, :]
    sums = []
    for k, win in enumerate(POOL_WINDOWS):
        shift = win // 2
        acc = acc + levels[k][s, row0 - shift:row0 - shift + n, :]
        sums.append(acc[:, :group])
        if k + 1 < len(POOL_WINDOWS):
            acc = acc[:, group:]
            levels[k + 1][s, row0:row0 + n, :] = acc
    return sums


def _mixer_tile(x_ref, y_ref, zstate_ref, pstate_ref, zbuf, levels, w, *,
                tile_index, segs, seg_rows, n_sub, start):
    d = x_ref.shape[-1]
    L = seg_rows
    M = segs * L
    group = d // len(POOL_WINDOWS)

    def project(r):
        hb = _rms_norm(x_ref[pl.ds(r * M, M), :], w["g_pre"]).astype(_BF16)
        return [_dot(hb, w["w_in"][:, k * d:(k + 1) * d]) for k in range(N_PROJ)]

    def mix(r, projections):
        pb, pc, pv, pp, pga, pgb = projections
        z = pc * pv
        for s in range(segs):
            zbuf[s, Z_HALO:Z_HALO + L, :] = z[s * L:(s + 1) * L]
        conv_w = w["conv_w"]
        ys = []
        for s in range(segs):
            y = conv_w[CONV_W - 1:CONV_W] * z[s * L:(s + 1) * L]
            for k in range(CONV_W - 1):
                off = Z_HALO - (CONV_W - 1 - k)
                y = y + conv_w[k:k + 1] * zbuf[s, off:off + L, :]
            ys.append(y)
        y = ys[0] if segs == 1 else jnp.concatenate(ys, axis=0)
        ya = (pb * y).astype(_BF16)
        for s in range(segs):
            levels[0][s, P_HALO:P_HALO + L, :] = pp[s * L:(s + 1) * L]
        sums = [_pool_window_sums(levels, s, P_HALO, L) for s in range(segs)]
        pos = start + (tile_index * n_sub + r) * L + lax.broadcasted_iota(jnp.int32, (L, 1), 0)
        dgs = []
        for gi, win in enumerate(POOL_WINDOWS):
            lanes = slice(gi * group, (gi + 1) * group)
            inv_cnt = 1.0 / jnp.minimum(pos + 1, win).astype(_F32)
            ds = [sums[s][gi] * inv_cnt - pp[s * L:(s + 1) * L, lanes] for s in range(segs)]
            dg = ds[0] if segs == 1 else jnp.concatenate(ds, axis=0)
            dgs.append(dg.astype(_BF16))
        for s in range(segs):
            zbuf[s, Z_HALO - CONV_HIST:Z_HALO, :] = zbuf[s, Z_HALO + L - CONV_HIST:Z_HALO + L, :]
            for buf in levels:
                buf[s, P_HALO - POOL_CARRY:P_HALO, :] = buf[s, P_HALO + L - POOL_CARRY:P_HALO + L, :]
        b_gate = w["b_gate"]
        g_a = jax.nn.sigmoid(pga + b_gate[:, :d])
        g_b = jax.nn.sigmoid(pgb + b_gate[:, d:])
        return ya, dgs, g_a, g_b

    def output(r, ya, dgs, g_a, g_b):
        branch_a = _dot(ya, w["w_conv_out"][...])
        yb = jnp.concatenate([_dot(dg, w["pool_w"][gi][...]) for gi, dg in enumerate(dgs)],
                             axis=-1)
        yb = (yb * w["pool_scale"]).astype(_BF16)
        branch_b = _dot(yb, w["w_pool_out"][...])
        merged = (g_a * branch_a + g_b * branch_b).astype(_BF16)
        o = _dot(merged, w["w_o"][...])
        rows = pl.ds(r * M, M)
        y_ref[rows, :] = x_ref[rows, :] + _rms_norm(o, w["g_post"])

    projections = project(0)
    for r in range(n_sub):
        upcoming = project(r + 1) if r + 1 < n_sub else None
        output(r, *mix(r, projections))
        projections = upcoming

    for s in range(segs):
        zstate_ref[s] = zbuf[s, Z_HALO - CONV_HIST:Z_HALO, :]
        pstate_ref[s] = levels[0][s, P_HALO - POOL_HIST:P_HALO, :]


def _mixer_kernel(xp_hbm, xs_ref, zhist_ref, phist_ref, b_gate_ref, conv_w_ref,
                  pool_scale_ref, g_pre_ref, g_post_ref,
                  w_in_hbm, w_conv_out_hbm, pool_w_hbm, w_pool_out_hbm, w_o_hbm,
                  yp_hbm, ys_ref, zstate_p_ref, pstate_p_ref, zstate_s_ref, pstate_s_ref,
                  w_in_v, w_conv_out_v, pool_w_v, w_pool_out_v, w_o_v,
                  step_ref, zbuf_p, zbuf_s, *level_bufs, layer, n_prompt_tiles, dec_rows):
    d = xs_ref.shape[-1]
    group = pool_w_v.shape[-1]
    levels_p = level_bufs[:len(POOL_WINDOWS)]
    levels_s = level_bufs[len(POOL_WINDOWS):]

    def zero_rows(buf, n):
        buf[:, 0:n, :] = jnp.zeros((buf.shape[0], n, buf.shape[2]), _F32)

    _load_weights_as_bf16(
        _column_blocks(w_in_hbm.at[layer], w_in_v, w_conv_out_v.shape[1]) + [
        (w_conv_out_hbm.at[layer], w_conv_out_v),
        (pool_w_hbm.at[layer], pool_w_v),
        (w_pool_out_hbm.at[layer], w_pool_out_v),
        (w_o_hbm.at[layer], w_o_v)])
    zero_rows(zbuf_p, Z_HALO)
    for buf in levels_p:
        zero_rows(buf, P_HALO)
    step_ref[0] = 0

    w = dict(
        w_in=w_in_v, w_conv_out=w_conv_out_v, w_pool_out=w_pool_out_v, w_o=w_o_v,
        pool_w=[pool_w_v.at[pl.ds(g * group, group)] for g in range(len(POOL_WINDOWS))],
        b_gate=b_gate_ref[layer:layer + 1, :], conv_w=conv_w_ref[layer],
        pool_scale=pool_scale_ref[layer:layer + 1, :],
        g_pre=g_pre_ref[layer:layer + 1, :], g_post=g_post_ref[layer:layer + 1, :])

    tile = MIXER_SUBTILES * PROMPT_SUB_ROWS

    def prompt_tile(x_ref, y_ref):
        i = step_ref[0]
        _mixer_tile(x_ref, y_ref, zstate_p_ref, pstate_p_ref, zbuf_p, levels_p, w,
                    tile_index=i, segs=1, seg_rows=PROMPT_SUB_ROWS, n_sub=MIXER_SUBTILES,
                    start=0)
        step_ref[0] = i + 1

    row_tiles = pl.BlockSpec((tile, d), lambda i: (i, 0))
    pltpu.emit_pipeline(prompt_tile, grid=(n_prompt_tiles,), in_specs=[row_tiles],
                        out_specs=[row_tiles])(xp_hbm, yp_hbm)

    segs = zhist_ref.shape[0]
    zbuf_s[:, Z_HALO - CONV_HIST:Z_HALO, :] = zhist_ref[...]
    for buf in levels_s:
        zero_rows(buf, P_HALO)
    levels_s[0][:, P_HALO - POOL_HIST:P_HALO, :] = phist_ref[...]
    for s in range(segs):
        _pool_window_sums(levels_s, s, P_HALO - POOL_CARRY, POOL_CARRY)
    _mixer_tile(xs_ref, ys_ref, zstate_s_ref, pstate_s_ref, zbuf_s, levels_s, w,
                tile_index=0, segs=segs, seg_rows=dec_rows, n_sub=1, start=PAST_LEN)


def _ffn_rows(x_ref, y_ref, w_up, w_down, g_pre, g_post, *, sub_rows, chunk):
    n_sub = x_ref.shape[0] // sub_rows
    n_chunks = w_up.shape[-1] // chunk
    hbs = [_rms_norm(x_ref[pl.ds(r * sub_rows, sub_rows), :], g_pre).astype(_BF16)
           for r in range(n_sub)]
    fs = [None] * n_sub
    for k in range(n_chunks):
        for r in range(n_sub):
            u = _dot(hbs[r], w_up[:, k * chunk:(k + 1) * chunk])
            a = jnp.square(jnp.maximum(u, 0.0)).astype(_BF16)
            part = _dot(a, w_down[k * chunk:(k + 1) * chunk, :])
            fs[r] = part if fs[r] is None else fs[r] + part
    for r in range(n_sub):
        rows = pl.ds(r * sub_rows, sub_rows)
        y_ref[rows, :] = x_ref[rows, :] + _rms_norm(fs[r], g_post)


def _ffn_kernel(xp_hbm, xs_ref, g_pre_ref, g_post_ref, w_up_hbm, w_down_hbm,
                yp_hbm, ys_ref, w_up_v, w_down_v, *, layer, n_prompt_tiles):
    d = xs_ref.shape[-1]
    _load_weights_as_bf16(_column_blocks(w_up_hbm.at[layer], w_up_v, d)
                          + [(w_down_hbm.at[layer], w_down_v)])
    g_pre = g_pre_ref[layer:layer + 1, :]
    g_post = g_post_ref[layer:layer + 1, :]

    def prompt_tile(x_ref, y_ref):
        _ffn_rows(x_ref, y_ref, w_up_v, w_down_v, g_pre, g_post,
                  sub_rows=PROMPT_SUB_ROWS, chunk=d)

    row_tiles = pl.BlockSpec((FFN_SUBTILES * PROMPT_SUB_ROWS, d), lambda i: (i, 0))
    pltpu.emit_pipeline(prompt_tile, grid=(n_prompt_tiles,), in_specs=[row_tiles],
                        out_specs=[row_tiles])(xp_hbm, yp_hbm)

    _ffn_rows(xs_ref, ys_ref, w_up_v, w_down_v, g_pre, g_post,
              sub_rows=xs_ref.shape[0], chunk=d)


_VMEM_SPEC = pl.BlockSpec(memory_space=pltpu.VMEM)
_ANY_SPEC = pl.BlockSpec(memory_space=pl.ANY)
_COMPILER_PARAMS = pltpu.CompilerParams(vmem_limit_bytes=VMEM_LIMIT_BYTES)


def _carry_buffers(segs, rows, d):
    group = d // len(POOL_WINDOWS)
    return ([pltpu.VMEM((segs, Z_HALO + rows, d), _F32)],
            [pltpu.VMEM((segs, P_HALO + rows, d - k * group), _F32)
             for k in range(len(POOL_WINDOWS))])


def _mixer_call(xp, xs, zhist, phist, p, *, layer, dec_rows):
    seq, d = xp.shape
    tile = MIXER_SUBTILES * PROMPT_SUB_ROWS
    n_tiles = seq // tile
    segs = zhist.shape[0]
    small = (p["b_gate"], p["conv_w"], p["pool_scale"], p["g_mix_pre"], p["g_mix_post"])
    big = (p["w_in"], p["w_conv_out"], p["pool_w"], p["w_pool_out"], p["w_o"])
    f32 = xp.dtype
    zbuf_p, levels_p = _carry_buffers(1, PROMPT_SUB_ROWS, d)
    zbuf_s, levels_s = _carry_buffers(segs, dec_rows, d)
    kern = functools.partial(_mixer_kernel, layer=layer, n_prompt_tiles=n_tiles,
                             dec_rows=dec_rows)
    return pl.pallas_call(
        kern,
        in_specs=[_ANY_SPEC] + [_VMEM_SPEC] * (3 + len(small)) + [_ANY_SPEC] * len(big),
        out_specs=[_ANY_SPEC] + [_VMEM_SPEC] * 5,
        out_shape=[jax.ShapeDtypeStruct(xp.shape, f32), jax.ShapeDtypeStruct(xs.shape, f32),
                   jax.ShapeDtypeStruct((1, CONV_HIST, d), f32),
                   jax.ShapeDtypeStruct((1, POOL_HIST, d), f32),
                   jax.ShapeDtypeStruct(zhist.shape, f32),
                   jax.ShapeDtypeStruct(phist.shape, f32)],
        scratch_shapes=[pltpu.VMEM(a.shape[1:], _BF16) for a in big]
                       + [pltpu.SMEM((1,), jnp.int32)]
                       + zbuf_p + zbuf_s + levels_p + levels_s,
        compiler_params=_COMPILER_PARAMS,
        name=f"mixer_l{layer}",
    )(xp, xs, zhist, phist, *small, *big)


def _ffn_call(xp, xs, p, *, layer):
    seq, d = xp.shape
    tile = FFN_SUBTILES * PROMPT_SUB_ROWS
    n_tiles = seq // tile
    small = (p["g_ffn_pre"], p["g_ffn_post"])
    big = (p["w_up"], p["w_down"])
    return pl.pallas_call(
        functools.partial(_ffn_kernel, layer=layer, n_prompt_tiles=n_tiles),
        in_specs=[_ANY_SPEC] + [_VMEM_SPEC] * (1 + len(small)) + [_ANY_SPEC] * len(big),
        out_specs=[_ANY_SPEC, _VMEM_SPEC],
        out_shape=[jax.ShapeDtypeStruct(xp.shape, xp.dtype),
                   jax.ShapeDtypeStruct(xs.shape, xs.dtype)],
        scratch_shapes=[pltpu.VMEM(a.shape[1:], _BF16) for a in big],
        compiler_params=_COMPILER_PARAMS,
        name=f"ffn_l{layer}",
    )(xp, xs, *small, *big)


def kernel(x_prompt, x_sample, cache_conv, cache_pool, w_in, b_gate, conv_w, w_conv_out,
           pool_w, pool_scale, w_pool_out, w_o, g_mix_pre, g_mix_post,
           w_up, w_down, g_ffn_pre, g_ffn_post):
    depth, d = g_mix_pre.shape
    batch, seq, _ = x_prompt.shape
    dec_batch, dec_seq, _ = x_sample.shape
    assert batch == 1
    assert seq % (PROMPT_SUB_ROWS * MIXER_SUBTILES) == 0
    assert seq % (PROMPT_SUB_ROWS * FFN_SUBTILES) == 0
    assert dec_seq >= POOL_HIST and dec_seq % SUBLANES == 0

    n_groups, group = pool_w.shape[1], pool_w.shape[2]
    params = dict(
        w_in=w_in, b_gate=b_gate, conv_w=conv_w, w_conv_out=w_conv_out,
        pool_w=pool_w.reshape(depth, n_groups * group, group), pool_scale=pool_scale,
        w_pool_out=w_pool_out, w_o=w_o, g_mix_pre=g_mix_pre, g_mix_post=g_mix_post,
        w_up=w_up, w_down=w_down, g_ffn_pre=g_ffn_pre, g_ffn_post=g_ffn_post)

    xp = x_prompt.reshape(seq, d)
    xs = x_sample.reshape(dec_batch * dec_seq, d)
    conv_p, pool_p, conv_s, pool_s = [], [], [], []
    for l in range(depth):
        xp, xs, zp, pp, zs, ps = _mixer_call(xp, xs, cache_conv[l], cache_pool[l], params,
                                             layer=l, dec_rows=dec_seq)
        xp, xs = _ffn_call(xp, xs, params, layer=l)
        conv_p.append(zp)
        pool_p.append(pp)
        conv_s.append(zs)
        pool_s.append(ps)
    return (xp.reshape(x_prompt.shape), xs.reshape(x_sample.shape),
            jnp.stack(conv_p), jnp.stack(pool_p), jnp.stack(conv_s), jnp.stack(pool_s))
```

```python
import functools

import jax
import jax.numpy as jnp
from jax import lax
from jax.experimental import pallas as pl
from jax.experimental.pallas import tpu as pltpu

EPS = 1e-6
CONV_W = 3
CONV_HIST = CONV_W - 1
POOL_WINDOWS = (2, 4, 8, 16)
POOL_HIST = max(POOL_WINDOWS) - 1
N_PROJ = 6
PAST_LEN = 2048

SUBLANES = 8
Z_HALO = -(-CONV_HIST // SUBLANES) * SUBLANES
POOL_CARRY = -(-max(POOL_WINDOWS) // SUBLANES) * SUBLANES
P_HALO = POOL_CARRY + SUBLANES
assert all(win == 2 ** (k + 1) for k, win in enumerate(POOL_WINDOWS))
assert POOL_HIST <= POOL_CARRY

PROMPT_SUB_ROWS = 256
MIXER_SUBTILES = 2
FFN_SUBTILES = 4
WEIGHT_STAGE_BYTES = 1024 * 1024
WEIGHT_COPIES_IN_FLIGHT = 7
BF16_SUBLANES = 2 * SUBLANES
ROUND_SLAB_ELEMS = 16 * 1024
VMEM_LIMIT_BYTES = 56 * 1024 * 1024

_F32 = jnp.float32
_BF16 = jnp.bfloat16


def _rms_norm(x, g):
    ms = jnp.mean(x * x, axis=-1, keepdims=True)
    return x * lax.rsqrt(ms + EPS) * g


def _dot(a, b):
    return jnp.dot(a, b, preferred_element_type=_F32)


def _stage_rows(n_rows, n_cols):
    rows = n_rows
    while rows * n_cols * 4 > WEIGHT_STAGE_BYTES and rows % (2 * SUBLANES) == 0:
        rows //= 2
    return rows


def _round_rows(src, dst):
    n_rows, n_cols = src.shape
    slab = BF16_SUBLANES
    while slab * n_cols < ROUND_SLAB_ELEMS and n_rows % (2 * slab) == 0:
        slab *= 2

    def round_slab(t, carry):
        rows = pl.ds(pl.multiple_of(t * slab, slab), slab)
        dst[rows, :] = src[rows, :].astype(_BF16)
        return carry

    lax.fori_loop(0, n_rows // slab, round_slab, 0)


def _column_blocks(src, dst, width):
    n_rows, n_cols = dst.shape
    return [(src.at[pl.ds(0, n_rows), pl.ds(c, width)], dst.at[pl.ds(0, n_rows), pl.ds(c, width)])
            for c in range(0, n_cols, width)]


def _load_weights_as_bf16(pairs):
    plan = []
    for src, _ in pairs:
        shape = (_stage_rows(*src.shape), src.shape[1])
        n_chunks = src.shape[0] // shape[0]
        for k, (other, count) in enumerate(plan):
            if other == shape:
                plan[k] = (shape, count + n_chunks)
                break
        else:
            plan.append((shape, n_chunks))
    shapes = [shape for shape, _ in plan]
    slots = [min(WEIGHT_COPIES_IN_FLIGHT + 1, count) for _, count in plan]

    def body(*scratch):
        stages, sems = scratch[:len(shapes)], scratch[len(shapes):]
        used = [0] * len(shapes)
        jobs = []
        for src, dst in pairs:
            rows = _stage_rows(*src.shape)
            which = shapes.index((rows, src.shape[1]))
            for k in range(src.shape[0] // rows):
                slot = used[which] % slots[which]
                used[which] += 1
                chunk = pl.ds(k * rows, rows)
                copy = pltpu.make_async_copy(
                    src.at[chunk], stages[which].at[slot], sems[which].at[slot])
                jobs.append((copy, stages[which], slot, dst, chunk))
        for copy, *_ in jobs[:WEIGHT_COPIES_IN_FLIGHT]:
            copy.start()
        for j, (copy, stage, slot, dst, chunk) in enumerate(jobs):
            if j + WEIGHT_COPIES_IN_FLIGHT < len(jobs):
                jobs[j + WEIGHT_COPIES_IN_FLIGHT][0].start()
            copy.wait()
            _round_rows(stage.at[slot], dst.at[chunk])

    pl.run_scoped(body,
                  *[pltpu.VMEM((n,) + shape, _F32) for n, shape in zip(slots, shapes)],
                  *[pltpu.SemaphoreType.DMA((n,)) for n in slots])


def _fold_pool_weights(pool_w_hbm, w_pool_out_hbm, scale, dst):
    n_groups = len(POOL_WINDOWS)
    group = pool_w_hbm.shape[-1]

    def body(pw_stage, out_stage, sems):
        copies = [pltpu.make_async_copy(pool_w_hbm, pw_stage, sems.at[0])]
        for g in range(n_groups):
            copies.append(pltpu.make_async_copy(
                w_pool_out_hbm.at[pl.ds(g * group, group)], out_stage.at[g], sems.at[1 + g]))
        for copy in copies:
            copy.start()
        copies[0].wait()
        for g in range(n_groups):
            copies[1 + g].wait()
            rows = pl.ds(g * group, group)
            scaled = pw_stage[rows, :] * scale[:, g * group:(g + 1) * group]
            dst[rows, :] = jnp.dot(scaled, out_stage[g], preferred_element_type=_F32,
                                   precision=lax.Precision.HIGHEST).astype(_BF16)

    pl.run_scoped(body,
                  pltpu.VMEM(pool_w_hbm.shape, _F32),
                  pltpu.VMEM((n_groups, group) + w_pool_out_hbm.shape[1:], _F32),
                  pltpu.SemaphoreType.DMA((1 + n_groups,)))


def _pool_window_sums(levels, s, row0, n):
    group = levels[0].shape[-1] // len(POOL_WINDOWS)
    acc = levels[0][s, row0:row0 + n, :]
    sums = []
    for k, win in enumerate(POOL_WINDOWS):
        shift = win // 2
        acc = acc + levels[k][s, row0 - shift:row0 - shift + n, :]
        sums.append(acc[:, :group])
        if k + 1 < len(POOL_WINDOWS):
            acc = acc[:, group:]
            levels[k + 1][s, row0:row0 + n, :] = acc
    return sums


def _mixer_tile(x_ref, y_ref, zstate_ref, pstate_ref, zbuf, levels, w, *,
                tile_index, segs, seg_rows, n_sub, start):
    d = x_ref.shape[-1]
    L = seg_rows
    M = segs * L
    group = d // len(POOL_WINDOWS)

    def project(r):
        hb = _rms_norm(x_ref[pl.ds(r * M, M), :], w["g_pre"]).astype(_BF16)
        return [_dot(hb, w["w_in"][:, k * d:(k + 1) * d]) for k in range(N_PROJ)]

    def mix(r, projections):
        pb, pc, pv, pp, pga, pgb = projections
        z = pc * pv
        for s in range(segs):
            zbuf[s, Z_HALO:Z_HALO + L, :] = z[s * L:(s + 1) * L]
        conv_w = w["conv_w"]
        ys = []
        for s in range(segs):
            y = conv_w[CONV_W - 1:CONV_W] * z[s * L:(s + 1) * L]
            for k in range(CONV_W - 1):
                off = Z_HALO - (CONV_W - 1 - k)
                y = y + conv_w[k:k + 1] * zbuf[s, off:off + L, :]
            ys.append(y)
        y = ys[0] if segs == 1 else jnp.concatenate(ys, axis=0)
        ya = (pb * y).astype(_BF16)
        for s in range(segs):
            levels[0][s, P_HALO:P_HALO + L, :] = pp[s * L:(s + 1) * L]
        sums = [_pool_window_sums(levels, s, P_HALO, L) for s in range(segs)]
        pos = start + (tile_index * n_sub + r) * L + lax.broadcasted_iota(jnp.int32, (L, 1), 0)
        dgs = []
        for gi, win in enumerate(POOL_WINDOWS):
            lanes = slice(gi * group, (gi + 1) * group)
            inv_cnt = 1.0 / jnp.minimum(pos + 1, win).astype(_F32)
            ds = [sums[s][gi] * inv_cnt - pp[s * L:(s + 1) * L, lanes] for s in range(segs)]
            dg = ds[0] if segs == 1 else jnp.concatenate(ds, axis=0)
            dgs.append(dg.astype(_BF16))
        for s in range(segs):
            zbuf[s, Z_HALO - CONV_HIST:Z_HALO, :] = zbuf[s, Z_HALO + L - CONV_HIST:Z_HALO + L, :]
            for buf in levels:
                buf[s, P_HALO - POOL_CARRY:P_HALO, :] = buf[s, P_HALO + L - POOL_CARRY:P_HALO + L, :]
        b_gate = w["b_gate"]
        g_a = jax.nn.sigmoid(pga + b_gate[:, :d])
        g_b = jax.nn.sigmoid(pgb + b_gate[:, d:])
        return ya, dgs, g_a, g_b

    def output(r, ya, dgs, g_a, g_b):
        branch_a = _dot(ya, w["w_conv_out"][...])
        branch_b = _dot(jnp.concatenate(dgs, axis=-1), w["pool_fold"][...])
        merged = (g_a * branch_a + g_b * branch_b).astype(_BF16)
        o = _dot(merged, w["w_o"][...])
        rows = pl.ds(r * M, M)
        y_ref[rows, :] = x_ref[rows, :] + _rms_norm(o, w["g_post"])

    projections = project(0)
    for r in range(n_sub):
        upcoming = project(r + 1) if r + 1 < n_sub else None
        output(r, *mix(r, projections))
        projections = upcoming

    for s in range(segs):
        zstate_ref[s] = zbuf[s, Z_HALO - CONV_HIST:Z_HALO, :]
        pstate_ref[s] = levels[0][s, P_HALO - POOL_HIST:P_HALO, :]


def _mixer_kernel(xp_ref, xs_ref, zhist_ref, phist_ref, b_gate_ref, conv_w_ref,
                  pool_scale_ref, g_pre_ref, g_post_ref,
                  w_in_hbm, w_conv_out_hbm, pool_w_hbm, w_pool_out_hbm, w_o_hbm,
                  yp_ref, ys_ref, zstate_p_ref, pstate_p_ref, zstate_s_ref, pstate_s_ref,
                  w_in_v, w_conv_out_v, pool_fold_v, w_o_v,
                  zbuf_p, zbuf_s, *level_bufs, layer, n_prompt_tiles, dec_rows):
    i = pl.program_id(0)
    levels_p = level_bufs[:len(POOL_WINDOWS)]
    levels_s = level_bufs[len(POOL_WINDOWS):]

    def zero_rows(buf, n):
        buf[:, 0:n, :] = jnp.zeros((buf.shape[0], n, buf.shape[2]), _F32)

    @pl.when(i == 0)
    def _first_step():
        _load_weights_as_bf16(
            _column_blocks(w_in_hbm.at[layer], w_in_v, w_conv_out_v.shape[1]) + [
            (w_conv_out_hbm.at[layer], w_conv_out_v),
            (w_o_hbm.at[layer], w_o_v)])
        _fold_pool_weights(pool_w_hbm.at[layer], w_pool_out_hbm.at[layer],
                           pool_scale_ref[layer:layer + 1, :], pool_fold_v)
        zero_rows(zbuf_p, Z_HALO)
        for buf in levels_p:
            zero_rows(buf, P_HALO)

    w = dict(
        w_in=w_in_v, w_conv_out=w_conv_out_v, pool_fold=pool_fold_v, w_o=w_o_v,
        b_gate=b_gate_ref[layer:layer + 1, :], conv_w=conv_w_ref[layer],
        g_pre=g_pre_ref[layer:layer + 1, :], g_post=g_post_ref[layer:layer + 1, :])

    @pl.when(i < n_prompt_tiles)
    def _prompt_tile():
        _mixer_tile(xp_ref, yp_ref, zstate_p_ref, pstate_p_ref, zbuf_p, levels_p, w,
                    tile_index=i, segs=1, seg_rows=PROMPT_SUB_ROWS, n_sub=MIXER_SUBTILES,
                    start=0)

    @pl.when(i == n_prompt_tiles)
    def _decode_rows():
        segs = zhist_ref.shape[0]
        zbuf_s[:, Z_HALO - CONV_HIST:Z_HALO, :] = zhist_ref[...]
        for buf in levels_s:
            zero_rows(buf, P_HALO)
        levels_s[0][:, P_HALO - POOL_HIST:P_HALO, :] = phist_ref[...]
        for s in range(segs):
            _pool_window_sums(levels_s, s, P_HALO - POOL_CARRY, POOL_CARRY)
        _mixer_tile(xs_ref, ys_ref, zstate_s_ref, pstate_s_ref, zbuf_s, levels_s, w,
                    tile_index=0, segs=segs, seg_rows=dec_rows, n_sub=1, start=PAST_LEN)


def _ffn_rows(x_ref, y_ref, w_up, w_down, g_pre, g_post, *, sub_rows, chunk):
    n_sub = x_ref.shape[0] // sub_rows
    n_chunks = w_up.shape[-1] // chunk
    hbs = [_rms_norm(x_ref[pl.ds(r * sub_rows, sub_rows), :], g_pre).astype(_BF16)
           for r in range(n_sub)]
    fs = [None] * n_sub
    for k in range(n_chunks):
        for r in range(n_sub):
            u = _dot(hbs[r], w_up[:, k * chunk:(k + 1) * chunk])
            a = jnp.square(jnp.maximum(u, 0.0)).astype(_BF16)
            part = _dot(a, w_down[k * chunk:(k + 1) * chunk, :])
            fs[r] = part if fs[r] is None else fs[r] + part
    for r in range(n_sub):
        rows = pl.ds(r * sub_rows, sub_rows)
        y_ref[rows, :] = x_ref[rows, :] + _rms_norm(fs[r], g_post)


def _ffn_kernel(xp_ref, xs_ref, g_pre_ref, g_post_ref, w_up_hbm, w_down_hbm,
                yp_ref, ys_ref, w_up_v, w_down_v, *, layer, n_prompt_tiles):
    i = pl.program_id(0)
    d = xp_ref.shape[-1]

    @pl.when(i == 0)
    def _first_step():
        _load_weights_as_bf16(_column_blocks(w_up_hbm.at[layer], w_up_v, d)
                              + [(w_down_hbm.at[layer], w_down_v)])

    g_pre = g_pre_ref[layer:layer + 1, :]
    g_post = g_post_ref[layer:layer + 1, :]

    @pl.when(i < n_prompt_tiles)
    def _prompt_tile():
        _ffn_rows(xp_ref, yp_ref, w_up_v, w_down_v, g_pre, g_post,
                  sub_rows=PROMPT_SUB_ROWS, chunk=d)

    @pl.when(i == n_prompt_tiles)
    def _decode_rows():
        _ffn_rows(xs_ref, ys_ref, w_up_v, w_down_v, g_pre, g_post,
                  sub_rows=xs_ref.shape[0], chunk=d)


def _full_spec(shape):
    zeros = (0,) * len(shape)
    return pl.BlockSpec(shape, lambda i: zeros)


_ANY_SPEC = pl.BlockSpec(memory_space=pl.ANY)
_COMPILER_PARAMS = pltpu.CompilerParams(
    dimension_semantics=("arbitrary",), vmem_limit_bytes=VMEM_LIMIT_BYTES)


def _carry_buffers(segs, rows, d):
    group = d // len(POOL_WINDOWS)
    return ([pltpu.VMEM((segs, Z_HALO + rows, d), _F32)],
            [pltpu.VMEM((segs, P_HALO + rows, d - k * group), _F32)
             for k in range(len(POOL_WINDOWS))])


def _mixer_call(xp, xs, zhist, phist, p, *, layer, dec_rows):
    seq, d = xp.shape
    tile = MIXER_SUBTILES * PROMPT_SUB_ROWS
    n_tiles = seq // tile
    segs = zhist.shape[0]
    small = (p["b_gate"], p["conv_w"], p["pool_scale"], p["g_mix_pre"], p["g_mix_post"])
    big = (p["w_in"], p["w_conv_out"], p["pool_w"], p["w_pool_out"], p["w_o"])
    prompt_spec = pl.BlockSpec((tile, d), lambda i: (jnp.minimum(i, n_tiles - 1), 0))
    f32 = xp.dtype
    zbuf_p, levels_p = _carry_buffers(1, PROMPT_SUB_ROWS, d)
    zbuf_s, levels_s = _carry_buffers(segs, dec_rows, d)
    kern = functools.partial(_mixer_kernel, layer=layer, n_prompt_tiles=n_tiles,
                             dec_rows=dec_rows)
    return pl.pallas_call(
        kern,
        grid=(n_tiles + 1,),
        in_specs=[prompt_spec, _full_spec(xs.shape), _full_spec(zhist.shape),
                  _full_spec(phist.shape)]
                 + [_full_spec(a.shape) for a in small] + [_ANY_SPEC] * len(big),
        out_specs=[prompt_spec, _full_spec(xs.shape),
                   _full_spec((1, CONV_HIST, d)), _full_spec((1, POOL_HIST, d)),
                   _full_spec(zhist.shape), _full_spec(phist.shape)],
        out_shape=[jax.ShapeDtypeStruct(xp.shape, f32), jax.ShapeDtypeStruct(xs.shape, f32),
                   jax.ShapeDtypeStruct((1, CONV_HIST, d), f32),
                   jax.ShapeDtypeStruct((1, POOL_HIST, d), f32),
                   jax.ShapeDtypeStruct(zhist.shape, f32),
                   jax.ShapeDtypeStruct(phist.shape, f32)],
        scratch_shapes=[pltpu.VMEM(p[name].shape[1:], _BF16)
                        for name in ("w_in", "w_conv_out", "w_pool_out", "w_o")]
                       + zbuf_p + zbuf_s + levels_p + levels_s,
        compiler_params=_COMPILER_PARAMS,
        name=f"mixer_l{layer}",
    )(xp, xs, zhist, phist, *small, *big)


def _ffn_call(xp, xs, p, *, layer):
    seq, d = xp.shape
    tile = FFN_SUBTILES * PROMPT_SUB_ROWS
    n_tiles = seq // tile
    small = (p["g_ffn_pre"], p["g_ffn_post"])
    big = (p["w_up"], p["w_down"])
    prompt_spec = pl.BlockSpec((tile, d), lambda i: (jnp.minimum(i, n_tiles - 1), 0))
    return pl.pallas_call(
        functools.partial(_ffn_kernel, layer=layer, n_prompt_tiles=n_tiles),
        grid=(n_tiles + 1,),
        in_specs=[prompt_spec, _full_spec(xs.shape)]
                 + [_full_spec(a.shape) for a in small] + [_ANY_SPEC] * len(big),
        out_specs=[prompt_spec, _full_spec(xs.shape)],
        out_shape=[jax.ShapeDtypeStruct(xp.shape, xp.dtype),
                   jax.ShapeDtypeStruct(xs.shape, xs.dtype)],
        scratch_shapes=[pltpu.VMEM(a.shape[1:], _BF16) for a in big],
        compiler_params=_COMPILER_PARAMS,
        name=f"ffn_l{layer}",
    )(xp, xs, *small, *big)


def kernel(x_prompt, x_sample, cache_conv, cache_pool, w_in, b_gate, conv_w, w_conv_out,
           pool_w, pool_scale, w_pool_out, w_o, g_mix_pre, g_mix_post,
           w_up, w_down, g_ffn_pre, g_ffn_post):
    depth, d = g_mix_pre.shape
    batch, seq, _ = x_prompt.shape
    dec_batch, dec_seq, _ = x_sample.shape
    assert batch == 1
    assert seq % (PROMPT_SUB_ROWS * MIXER_SUBTILES) == 0
    assert seq % (PROMPT_SUB_ROWS * FFN_SUBTILES) == 0
    assert dec_seq >= POOL_HIST and dec_seq % SUBLANES == 0

    n_groups, group = pool_w.shape[1], pool_w.shape[2]
    params = dict(
        w_in=w_in, b_gate=b_gate, conv_w=conv_w, w_conv_out=w_conv_out,
        pool_w=pool_w.reshape(depth, n_groups * group, group), pool_scale=pool_scale,
        w_pool_out=w_pool_out, w_o=w_o, g_mix_pre=g_mix_pre, g_mix_post=g_mix_post,
        w_up=w_up, w_down=w_down, g_ffn_pre=g_ffn_pre, g_ffn_post=g_ffn_post)

    xp = x_prompt.reshape(seq, d)
    xs = x_sample.reshape(dec_batch * dec_seq, d)
    conv_p, pool_p, conv_s, pool_s = [], [], [], []
    for l in range(depth):
        xp, xs, zp, pp, zs, ps = _mixer_call(xp, xs, cache_conv[l], cache_pool[l], params,
                                             layer=l, dec_rows=dec_seq)
        xp, xs = _ffn_call(xp, xs, params, layer=l)
        conv_p.append(zp)
        pool_p.append(pp)
        conv_s.append(zs)
        pool_s.append(ps)
    return (xp.reshape(x_prompt.shape), xs.reshape(x_sample.shape),
            jnp.stack(conv_p), jnp.stack(pool_p), jnp.stack(conv_s), jnp.stack(pool_s))
```

```python
import functools

import jax
import jax.numpy as jnp
from jax import lax
from jax.experimental import pallas as pl
from jax.experimental.pallas import tpu as pltpu

EPS = 1e-6
CONV_W = 3
CONV_HIST = CONV_W - 1
POOL_WINDOWS = (2, 4, 8, 16)
POOL_HIST = max(POOL_WINDOWS) - 1
N_PROJ = 6
PAST_LEN = 2048

SUBLANES = 8
Z_HALO = -(-CONV_HIST // SUBLANES) * SUBLANES
POOL_CARRY = -(-max(POOL_WINDOWS) // SUBLANES) * SUBLANES
P_HALO = POOL_CARRY + SUBLANES
assert all(win == 2 ** (k + 1) for k, win in enumerate(POOL_WINDOWS))
assert POOL_HIST <= POOL_CARRY

PROMPT_SUB_ROWS = 256
MIXER_SUBTILES = 2
FFN_SUBTILES = 4
WEIGHT_STAGE_BYTES = 1024 * 1024
WEIGHT_COPIES_IN_FLIGHT = 7
BF16_SUBLANES = 2 * SUBLANES
ROUND_SLAB_ELEMS = 16 * 1024
VMEM_LIMIT_BYTES = 56 * 1024 * 1024

_F32 = jnp.float32
_BF16 = jnp.bfloat16


def _rms_norm(x, g):
    ms = jnp.mean(x * x, axis=-1, keepdims=True)
    return x * lax.rsqrt(ms + EPS) * g


def _dot(a, b):
    return jnp.dot(a, b, preferred_element_type=_F32)


def _stage_rows(n_rows, n_cols):
    rows = n_rows
    while rows * n_cols * 4 > WEIGHT_STAGE_BYTES and rows % (2 * SUBLANES) == 0:
        rows //= 2
    return rows


def _round_rows(src, dst):
    n_rows, n_cols = src.shape
    slab = BF16_SUBLANES
    while slab * n_cols < ROUND_SLAB_ELEMS and n_rows % (2 * slab) == 0:
        slab *= 2

    def round_slab(t, carry):
        rows = pl.ds(pl.multiple_of(t * slab, slab), slab)
        dst[rows, :] = src[rows, :].astype(_BF16)
        return carry

    lax.fori_loop(0, n_rows // slab, round_slab, 0)


def _column_blocks(src, dst, width):
    n_rows, n_cols = dst.shape
    return [(src.at[pl.ds(0, n_rows), pl.ds(c, width)], dst.at[pl.ds(0, n_rows), pl.ds(c, width)])
            for c in range(0, n_cols, width)]


def _load_weights_as_bf16(pairs):
    plan = []
    for src, _ in pairs:
        shape = (_stage_rows(*src.shape), src.shape[1])
        n_chunks = src.shape[0] // shape[0]
        for k, (other, count) in enumerate(plan):
            if other == shape:
                plan[k] = (shape, count + n_chunks)
                break
        else:
            plan.append((shape, n_chunks))
    shapes = [shape for shape, _ in plan]
    slots = [min(WEIGHT_COPIES_IN_FLIGHT + 1, count) for _, count in plan]

    def body(*scratch):
        stages, sems = scratch[:len(shapes)], scratch[len(shapes):]
        used = [0] * len(shapes)
        jobs = []
        for src, dst in pairs:
            rows = _stage_rows(*src.shape)
            which = shapes.index((rows, src.shape[1]))
            for k in range(src.shape[0] // rows):
                slot = used[which] % slots[which]
                used[which] += 1
                chunk = pl.ds(k * rows, rows)
                copy = pltpu.make_async_copy(
                    src.at[chunk], stages[which].at[slot], sems[which].at[slot])
                jobs.append((copy, stages[which], slot, dst, chunk))
        for copy, *_ in jobs[:WEIGHT_COPIES_IN_FLIGHT]:
            copy.start()
        for j, (copy, stage, slot, dst, chunk) in enumerate(jobs):
            if j + WEIGHT_COPIES_IN_FLIGHT < len(jobs):
                jobs[j + WEIGHT_COPIES_IN_FLIGHT][0].start()
            copy.wait()
            _round_rows(stage.at[slot], dst.at[chunk])

    pl.run_scoped(body,
                  *[pltpu.VMEM((n,) + shape, _F32) for n, shape in zip(slots, shapes)],
                  *[pltpu.SemaphoreType.DMA((n,)) for n in slots])


def _dot_split3(a, b):
    a_hi, b_hi = a.astype(_BF16), b.astype(_BF16)
    a_lo = (a - a_hi.astype(_F32)).astype(_BF16)
    b_lo = (b - b_hi.astype(_F32)).astype(_BF16)
    return _dot(a_hi, b_hi) + _dot(a_hi, b_lo) + _dot(a_lo, b_hi)


def _fold_pool_weights(pool_w_hbm, w_pool_out_hbm, scale, dst):
    n_groups = len(POOL_WINDOWS)
    group = pool_w_hbm.shape[-1]

    def body(pw_stage, out_stage, sems):
        copies = [pltpu.make_async_copy(pool_w_hbm, pw_stage, sems.at[0])]
        for g in range(n_groups):
            copies.append(pltpu.make_async_copy(
                w_pool_out_hbm.at[pl.ds(g * group, group)], out_stage.at[g], sems.at[1 + g]))
        for copy in copies:
            copy.start()
        copies[0].wait()
        for g in range(n_groups):
            copies[1 + g].wait()
            rows = pl.ds(g * group, group)
            scaled = pw_stage[rows, :] * scale[:, g * group:(g + 1) * group]
            dst[rows, :] = _dot_split3(scaled, out_stage[g]).astype(_BF16)

    pl.run_scoped(body,
                  pltpu.VMEM(pool_w_hbm.shape, _F32),
                  pltpu.VMEM((n_groups, group) + w_pool_out_hbm.shape[1:], _F32),
                  pltpu.SemaphoreType.DMA((1 + n_groups,)))


def _pool_window_sums(levels, s, row0, n):
    group = levels[0].shape[-1] // len(POOL_WINDOWS)
    acc = levels[0][s, row0:row0 + n, :]
    sums = []
    for k, win in enumerate(POOL_WINDOWS):
        shift = win // 2
        acc = acc + levels[k][s, row0 - shift:row0 - shift + n, :]
        sums.append(acc[:, :group])
        if k + 1 < len(POOL_WINDOWS):
            acc = acc[:, group:]
            levels[k + 1][s, row0:row0 + n, :] = acc
    return sums


def _mixer_tile(x_ref, y_ref, zstate_ref, pstate_ref, zbuf, levels, w, *,
                tile_index, segs, seg_rows, n_sub, start):
    d = x_ref.shape[-1]
    L = seg_rows
    M = segs * L
    group = d // len(POOL_WINDOWS)

    def project(r):
        hb = _rms_norm(x_ref[pl.ds(r * M, M), :], w["g_pre"]).astype(_BF16)
        return [_dot(hb, w["w_in"][:, k * d:(k + 1) * d]) for k in range(N_PROJ)]

    def mix(r, projections):
        pb, pc, pv, pp, pga, pgb = projections
        z = pc * pv
        for s in range(segs):
            zbuf[s, Z_HALO:Z_HALO + L, :] = z[s * L:(s + 1) * L]
        conv_w = w["conv_w"]
        ys = []
        for s in range(segs):
            y = conv_w[CONV_W - 1:CONV_W] * z[s * L:(s + 1) * L]
            for k in range(CONV_W - 1):
                off = Z_HALO - (CONV_W - 1 - k)
                y = y + conv_w[k:k + 1] * zbuf[s, off:off + L, :]
            ys.append(y)
        y = ys[0] if segs == 1 else jnp.concatenate(ys, axis=0)
        ya = (pb * y).astype(_BF16)
        for s in range(segs):
            levels[0][s, P_HALO:P_HALO + L, :] = pp[s * L:(s + 1) * L]
        sums = [_pool_window_sums(levels, s, P_HALO, L) for s in range(segs)]
        pos = start + (tile_index * n_sub + r) * L + lax.broadcasted_iota(jnp.int32, (L, 1), 0)
        dgs = []
        for gi, win in enumerate(POOL_WINDOWS):
            lanes = slice(gi * group, (gi + 1) * group)
            inv_cnt = 1.0 / jnp.minimum(pos + 1, win).astype(_F32)
            ds = [sums[s][gi] * inv_cnt - pp[s * L:(s + 1) * L, lanes] for s in range(segs)]
            dg = ds[0] if segs == 1 else jnp.concatenate(ds, axis=0)
            dgs.append(dg.astype(_BF16))
        for s in range(segs):
            zbuf[s, Z_HALO - CONV_HIST:Z_HALO, :] = zbuf[s, Z_HALO + L - CONV_HIST:Z_HALO + L, :]
            for buf in levels:
                buf[s, P_HALO - POOL_CARRY:P_HALO, :] = buf[s, P_HALO + L - POOL_CARRY:P_HALO + L, :]
        b_gate = w["b_gate"]
        g_a = jax.nn.sigmoid(pga + b_gate[:, :d])
        g_b = jax.nn.sigmoid(pgb + b_gate[:, d:])
        return ya, dgs, g_a, g_b

    def output(r, ya, dgs, g_a, g_b):
        branch_a = _dot(ya, w["w_conv_out"][...])
        branch_b = _dot(jnp.concatenate(dgs, axis=-1), w["pool_fold"][...])
        merged = (g_a * branch_a + g_b * branch_b).astype(_BF16)
        o = _dot(merged, w["w_o"][...])
        rows = pl.ds(r * M, M)
        y_ref[rows, :] = x_ref[rows, :] + _rms_norm(o, w["g_post"])

    projections = project(0)
    for r in range(n_sub):
        upcoming = project(r + 1) if r + 1 < n_sub else None
        output(r, *mix(r, projections))
        projections = upcoming

    for s in range(segs):
        zstate_ref[s] = zbuf[s, Z_HALO - CONV_HIST:Z_HALO, :]
        pstate_ref[s] = levels[0][s, P_HALO - POOL_HIST:P_HALO, :]


def _mixer_kernel(xp_ref, xs_ref, zhist_ref, phist_ref, b_gate_ref, conv_w_ref,
                  pool_scale_ref, g_pre_ref, g_post_ref,
                  w_in_hbm, w_conv_out_hbm, pool_w_hbm, w_pool_out_hbm, w_o_hbm,
                  yp_ref, ys_ref, zstate_p_ref, pstate_p_ref, zstate_s_ref, pstate_s_ref,
                  w_in_v, w_conv_out_v, pool_fold_v, w_o_v,
                  zbuf_p, zbuf_s, *level_bufs, layer, n_prompt_tiles, dec_rows):
    i = pl.program_id(0)
    levels_p = level_bufs[:len(POOL_WINDOWS)]
    levels_s = level_bufs[len(POOL_WINDOWS):]

    def zero_rows(buf, n):
        buf[:, 0:n, :] = jnp.zeros((buf.shape[0], n, buf.shape[2]), _F32)

    @pl.when(i == 0)
    def _first_step():
        _load_weights_as_bf16(
            _column_blocks(w_in_hbm.at[layer], w_in_v, w_conv_out_v.shape[1]) + [
            (w_conv_out_hbm.at[layer], w_conv_out_v),
            (w_o_hbm.at[layer], w_o_v)])
        _fold_pool_weights(pool_w_hbm.at[layer], w_pool_out_hbm.at[layer],
                           pool_scale_ref[layer:layer + 1, :], pool_fold_v)
        zero_rows(zbuf_p, Z_HALO)
        for buf in levels_p:
            zero_rows(buf, P_HALO)

    w = dict(
        w_in=w_in_v, w_conv_out=w_conv_out_v, pool_fold=pool_fold_v, w_o=w_o_v,
        b_gate=b_gate_ref[layer:layer + 1, :], conv_w=conv_w_ref[layer],
        g_pre=g_pre_ref[layer:layer + 1, :], g_post=g_post_ref[layer:layer + 1, :])

    @pl.when(i < n_prompt_tiles)
    def _prompt_tile():
        _mixer_tile(xp_ref, yp_ref, zstate_p_ref, pstate_p_ref, zbuf_p, levels_p, w,
                    tile_index=i, segs=1, seg_rows=PROMPT_SUB_ROWS, n_sub=MIXER_SUBTILES,
                    start=0)

    @pl.when(i == n_prompt_tiles)
    def _decode_rows():
        segs = zhist_ref.shape[0]
        zbuf_s[:, Z_HALO - CONV_HIST:Z_HALO, :] = zhist_ref[...]
        for buf in levels_s:
            zero_rows(buf, P_HALO)
        levels_s[0][:, P_HALO - POOL_HIST:P_HALO, :] = phist_ref[...]
        for s in range(segs):
            _pool_window_sums(levels_s, s, P_HALO - POOL_CARRY, POOL_CARRY)
        _mixer_tile(xs_ref, ys_ref, zstate_s_ref, pstate_s_ref, zbuf_s, levels_s, w,
                    tile_index=0, segs=segs, seg_rows=dec_rows, n_sub=1, start=PAST_LEN)


def _ffn_rows(x_ref, y_ref, w_up, w_down, g_pre, g_post, *, sub_rows, chunk):
    n_sub = x_ref.shape[0] // sub_rows
    n_chunks = w_up.shape[-1] // chunk
    hbs = [_rms_norm(x_ref[pl.ds(r * sub_rows, sub_rows), :], g_pre).astype(_BF16)
           for r in range(n_sub)]
    fs = [None] * n_sub
    for k in range(n_chunks):
        for r in range(n_sub):
            u = _dot(hbs[r], w_up[:, k * chunk:(k + 1) * chunk])
            a = jnp.square(jnp.maximum(u, 0.0)).astype(_BF16)
            part = _dot(a, w_down[k * chunk:(k + 1) * chunk, :])
            fs[r] = part if fs[r] is None else fs[r] + part
    for r in range(n_sub):
        rows = pl.ds(r * sub_rows, sub_rows)
        y_ref[rows, :] = x_ref[rows, :] + _rms_norm(fs[r], g_post)


def _ffn_kernel(xp_ref, xs_ref, g_pre_ref, g_post_ref, w_up_hbm, w_down_hbm,
                yp_ref, ys_ref, w_up_v, w_down_v, *, layer, n_prompt_tiles):
    i = pl.program_id(0)
    d = xp_ref.shape[-1]

    @pl.when(i == 0)
    def _first_step():
        _load_weights_as_bf16(_column_blocks(w_up_hbm.at[layer], w_up_v, d)
                              + [(w_down_hbm.at[layer], w_down_v)])

    g_pre = g_pre_ref[layer:layer + 1, :]
    g_post = g_post_ref[layer:layer + 1, :]

    @pl.when(i < n_prompt_tiles)
    def _prompt_tile():
        _ffn_rows(xp_ref, yp_ref, w_up_v, w_down_v, g_pre, g_post,
                  sub_rows=PROMPT_SUB_ROWS, chunk=d)

    @pl.when(i == n_prompt_tiles)
    def _decode_rows():
        _ffn_rows(xs_ref, ys_ref, w_up_v, w_down_v, g_pre, g_post,
                  sub_rows=xs_ref.shape[0], chunk=d)


def _full_spec(shape):
    zeros = (0,) * len(shape)
    return pl.BlockSpec(shape, lambda i: zeros)


_ANY_SPEC = pl.BlockSpec(memory_space=pl.ANY)
_COMPILER_PARAMS = pltpu.CompilerParams(
    dimension_semantics=("arbitrary",), vmem_limit_bytes=VMEM_LIMIT_BYTES)


def _carry_buffers(segs, rows, d):
    group = d // len(POOL_WINDOWS)
    return ([pltpu.VMEM((segs, Z_HALO + rows, d), _F32)],
            [pltpu.VMEM((segs, P_HALO + rows, d - k * group), _F32)
             for k in range(len(POOL_WINDOWS))])


def _mixer_call(xp, xs, zhist, phist, p, *, layer, dec_rows):
    seq, d = xp.shape
    tile = MIXER_SUBTILES * PROMPT_SUB_ROWS
    n_tiles = seq // tile
    segs = zhist.shape[0]
    small = (p["b_gate"], p["conv_w"], p["pool_scale"], p["g_mix_pre"], p["g_mix_post"])
    big = (p["w_in"], p["w_conv_out"], p["pool_w"], p["w_pool_out"], p["w_o"])
    prompt_spec = pl.BlockSpec((tile, d), lambda i: (jnp.minimum(i, n_tiles - 1), 0))
    f32 = xp.dtype
    zbuf_p, levels_p = _carry_buffers(1, PROMPT_SUB_ROWS, d)
    zbuf_s, levels_s = _carry_buffers(segs, dec_rows, d)
    kern = functools.partial(_mixer_kernel, layer=layer, n_prompt_tiles=n_tiles,
                             dec_rows=dec_rows)
    return pl.pallas_call(
        kern,
        grid=(n_tiles + 1,),
        in_specs=[prompt_spec, _full_spec(xs.shape), _full_spec(zhist.shape),
                  _full_spec(phist.shape)]
                 + [_full_spec(a.shape) for a in small] + [_ANY_SPEC] * len(big),
        out_specs=[prompt_spec, _full_spec(xs.shape),
                   _full_spec((1, CONV_HIST, d)), _full_spec((1, POOL_HIST, d)),
                   _full_spec(zhist.shape), _full_spec(phist.shape)],
        out_shape=[jax.ShapeDtypeStruct(xp.shape, f32), jax.ShapeDtypeStruct(xs.shape, f32),
                   jax.ShapeDtypeStruct((1, CONV_HIST, d), f32),
                   jax.ShapeDtypeStruct((1, POOL_HIST, d), f32),
                   jax.ShapeDtypeStruct(zhist.shape, f32),
                   jax.ShapeDtypeStruct(phist.shape, f32)],
        scratch_shapes=[pltpu.VMEM(p[name].shape[1:], _BF16)
                        for name in ("w_in", "w_conv_out", "w_pool_out", "w_o")]
                       + zbuf_p + zbuf_s + levels_p + levels_s,
        compiler_params=_COMPILER_PARAMS,
        name=f"mixer_l{layer}",
    )(xp, xs, zhist, phist, *small, *big)


def _ffn_call(xp, xs, p, *, layer):
    seq, d = xp.shape
    tile = FFN_SUBTILES * PROMPT_SUB_ROWS
    n_tiles = seq // tile
    small = (p["g_ffn_pre"], p["g_ffn_post"])
    big = (p["w_up"], p["w_down"])
    prompt_spec = pl.BlockSpec((tile, d), lambda i: (jnp.minimum(i, n_tiles - 1), 0))
    return pl.pallas_call(
        functools.partial(_ffn_kernel, layer=layer, n_prompt_tiles=n_tiles),
        grid=(n_tiles + 1,),
        in_specs=[prompt_spec, _full_spec(xs.shape)]
                 + [_full_spec(a.shape) for a in small] + [_ANY_SPEC] * len(big),
        out_specs=[prompt_spec, _full_spec(xs.shape)],
        out_shape=[jax.ShapeDtypeStruct(xp.shape, xp.dtype),
                   jax.ShapeDtypeStruct(xs.shape, xs.dtype)],
        scratch_shapes=[pltpu.VMEM(a.shape[1:], _BF16) for a in big],
        compiler_params=_COMPILER_PARAMS,
        name=f"ffn_l{layer}",
    )(xp, xs, *small, *big)


def kernel(x_prompt, x_sample, cache_conv, cache_pool, w_in, b_gate, conv_w, w_conv_out,
           pool_w, pool_scale, w_pool_out, w_o, g_mix_pre, g_mix_post,
           w_up, w_down, g_ffn_pre, g_ffn_post):
    depth, d = g_mix_pre.shape
    batch, seq, _ = x_prompt.shape
    dec_batch, dec_seq, _ = x_sample.shape
    assert batch == 1
    assert seq % (PROMPT_SUB_ROWS * MIXER_SUBTILES) == 0
    assert seq % (PROMPT_SUB_ROWS * FFN_SUBTILES) == 0
    assert dec_seq >= POOL_HIST and dec_seq % SUBLANES == 0

    n_groups, group = pool_w.shape[1], pool_w.shape[2]
    params = dict(
        w_in=w_in, b_gate=b_gate, conv_w=conv_w, w_conv_out=w_conv_out,
        pool_w=pool_w.reshape(depth, n_groups * group, group), pool_scale=pool_scale,
        w_pool_out=w_pool_out, w_o=w_o, g_mix_pre=g_mix_pre, g_mix_post=g_mix_post,
        w_up=w_up, w_down=w_down, g_ffn_pre=g_ffn_pre, g_ffn_post=g_ffn_post)

    xp = x_prompt.reshape(seq, d)
    xs = x_sample.reshape(dec_batch * dec_seq, d)
    conv_p, pool_p, conv_s, pool_s = [], [], [], []
    for l in range(depth):
        xp, xs, zp, pp, zs, ps = _mixer_call(xp, xs, cache_conv[l], cache_pool[l], params,
                                             layer=l, dec_rows=dec_seq)
        xp, xs = _ffn_call(xp, xs, params, layer=l)
        conv_p.append(zp)
        pool_p.append(pp)
        conv_s.append(zs)
        pool_s.append(ps)
    return (xp.reshape(x_prompt.shape), xs.reshape(x_sample.shape),
            jnp.stack(conv_p), jnp.stack(pool_p), jnp.stack(conv_s), jnp.stack(pool_s))
```

```python
import functools

import jax
import jax.numpy as jnp
from jax import lax
from jax.experimental import pallas as pl
from jax.experimental.pallas import tpu as pltpu

EPS = 1e-6
CONV_W = 3
CONV_HIST = CONV_W - 1
POOL_WINDOWS = (2, 4, 8, 16)
POOL_HIST = max(POOL_WINDOWS) - 1
N_PROJ = 6
PAST_LEN = 2048

SUBLANES = 8
Z_HALO = -(-CONV_HIST // SUBLANES) * SUBLANES
POOL_CARRY = -(-max(POOL_WINDOWS) // SUBLANES) * SUBLANES
P_HALO = POOL_CARRY + SUBLANES
assert all(win == 2 ** (k + 1) for k, win in enumerate(POOL_WINDOWS))
assert POOL_HIST <= POOL_CARRY

PROMPT_SUB_ROWS = 256
MIXER_SUBTILES = 2
FFN_SUBTILES = 4
WEIGHT_STAGE_BYTES = 1024 * 1024
WEIGHT_COPIES_IN_FLIGHT = 7
BF16_SUBLANES = 2 * SUBLANES
ROUND_SLAB_ELEMS = 16 * 1024
VMEM_LIMIT_BYTES = 56 * 1024 * 1024

_F32 = jnp.float32
_BF16 = jnp.bfloat16


def _rms_norm(x, g):
    ms = jnp.mean(x * x, axis=-1, keepdims=True)
    return x * lax.rsqrt(ms + EPS) * g


def _dot(a, b):
    return jnp.dot(a, b, preferred_element_type=_F32)


def _stage_rows(n_rows, n_cols):
    rows = n_rows
    while rows * n_cols * 4 > WEIGHT_STAGE_BYTES and rows % (2 * SUBLANES) == 0:
        rows //= 2
    return rows


def _round_rows(src, dst):
    n_rows, n_cols = src.shape
    slab = BF16_SUBLANES
    while slab * n_cols < ROUND_SLAB_ELEMS and n_rows % (2 * slab) == 0:
        slab *= 2

    def round_slab(t, carry):
        rows = pl.ds(pl.multiple_of(t * slab, slab), slab)
        dst[rows, :] = src[rows, :].astype(_BF16)
        return carry

    lax.fori_loop(0, n_rows // slab, round_slab, 0)


def _column_blocks(src, dst, width):
    n_rows, n_cols = dst.shape
    return [(src.at[pl.ds(0, n_rows), pl.ds(c, width)], dst.at[pl.ds(0, n_rows), pl.ds(c, width)])
            for c in range(0, n_cols, width)]


def _load_weights_as_bf16(pairs, between_chunks=()):
    plan = []
    for src, _ in pairs:
        shape = (_stage_rows(*src.shape), src.shape[1])
        n_chunks = src.shape[0] // shape[0]
        for k, (other, count) in enumerate(plan):
            if other == shape:
                plan[k] = (shape, count + n_chunks)
                break
        else:
            plan.append((shape, n_chunks))
    shapes = [shape for shape, _ in plan]
    slots = [min(WEIGHT_COPIES_IN_FLIGHT + 1, count) for _, count in plan]

    def body(*scratch):
        stages, sems = scratch[:len(shapes)], scratch[len(shapes):]
        used = [0] * len(shapes)
        jobs = []
        for src, dst in pairs:
            rows = _stage_rows(*src.shape)
            which = shapes.index((rows, src.shape[1]))
            for k in range(src.shape[0] // rows):
                slot = used[which] % slots[which]
                used[which] += 1
                chunk = pl.ds(k * rows, rows)
                copy = pltpu.make_async_copy(
                    src.at[chunk], stages[which].at[slot], sems[which].at[slot])
                jobs.append((copy, stages[which], slot, dst, chunk))
        for copy, *_ in jobs[:WEIGHT_COPIES_IN_FLIGHT]:
            copy.start()
        stride = len(jobs) // (len(between_chunks) + 1)
        for j, (copy, stage, slot, dst, chunk) in enumerate(jobs):
            if j + WEIGHT_COPIES_IN_FLIGHT < len(jobs):
                jobs[j + WEIGHT_COPIES_IN_FLIGHT][0].start()
            copy.wait()
            _round_rows(stage.at[slot], dst.at[chunk])
            if (j + 1) % stride == 0 and (j + 1) // stride <= len(between_chunks):
                between_chunks[(j + 1) // stride - 1]()

    pl.run_scoped(body,
                  *[pltpu.VMEM((n,) + shape, _F32) for n, shape in zip(slots, shapes)],
                  *[pltpu.SemaphoreType.DMA((n,)) for n in slots])


def _dot_split3(a, b):
    a_hi, b_hi = a.astype(_BF16), b.astype(_BF16)
    a_lo = (a - a_hi.astype(_F32)).astype(_BF16)
    b_lo = (b - b_hi.astype(_F32)).astype(_BF16)
    return _dot(a_hi, b_hi) + _dot(a_hi, b_lo) + _dot(a_lo, b_hi)


def _fold_pool_weights(pool_w_hbm, w_pool_out_hbm, scale, dst, load_other_weights):
    n_groups = len(POOL_WINDOWS)
    group = pool_w_hbm.shape[-1]

    def body(pw_stage, out_stage, sems):
        copies = [pltpu.make_async_copy(pool_w_hbm, pw_stage, sems.at[0])]
        for g in range(n_groups):
            copies.append(pltpu.make_async_copy(
                w_pool_out_hbm.at[pl.ds(g * group, group)], out_stage.at[g], sems.at[1 + g]))
        for copy in copies:
            copy.start()

        def fold_group(g):
            if g == 0:
                copies[0].wait()
            copies[1 + g].wait()
            rows = pl.ds(g * group, group)
            scaled = pw_stage[rows, :] * scale[:, g * group:(g + 1) * group]
            dst[rows, :] = _dot_split3(scaled, out_stage[g]).astype(_BF16)

        load_other_weights([functools.partial(fold_group, g) for g in range(n_groups)])

    pl.run_scoped(body,
                  pltpu.VMEM(pool_w_hbm.shape, _F32),
                  pltpu.VMEM((n_groups, group) + w_pool_out_hbm.shape[1:], _F32),
                  pltpu.SemaphoreType.DMA((1 + n_groups,)))


def _pool_window_sums(levels, s, row0, n):
    group = levels[0].shape[-1] // len(POOL_WINDOWS)
    acc = levels[0][s, row0:row0 + n, :]
    sums = []
    for k, win in enumerate(POOL_WINDOWS):
        shift = win // 2
        acc = acc + levels[k][s, row0 - shift:row0 - shift + n, :]
        sums.append(acc[:, :group])
        if k + 1 < len(POOL_WINDOWS):
            acc = acc[:, group:]
            levels[k + 1][s, row0:row0 + n, :] = acc
    return sums


def _mixer_tile(x_ref, y_ref, zstate_ref, pstate_ref, zbuf, levels, w, *,
                tile_index, segs, seg_rows, n_sub, start):
    d = x_ref.shape[-1]
    L = seg_rows
    M = segs * L
    group = d // len(POOL_WINDOWS)

    def project(r):
        hb = _rms_norm(x_ref[pl.ds(r * M, M), :], w["g_pre"]).astype(_BF16)
        return [_dot(hb, w["w_in"][:, k * d:(k + 1) * d]) for k in range(N_PROJ)]

    def mix(r, projections):
        pb, pc, pv, pp, pga, pgb = projections
        z = pc * pv
        for s in range(segs):
            zbuf[s, Z_HALO:Z_HALO + L, :] = z[s * L:(s + 1) * L]
        conv_w = w["conv_w"]
        ys = []
        for s in range(segs):
            y = conv_w[CONV_W - 1:CONV_W] * z[s * L:(s + 1) * L]
            for k in range(CONV_W - 1):
                off = Z_HALO - (CONV_W - 1 - k)
                y = y + conv_w[k:k + 1] * zbuf[s, off:off + L, :]
            ys.append(y)
        y = ys[0] if segs == 1 else jnp.concatenate(ys, axis=0)
        ya = (pb * y).astype(_BF16)
        for s in range(segs):
            levels[0][s, P_HALO:P_HALO + L, :] = pp[s * L:(s + 1) * L]
        sums = [_pool_window_sums(levels, s, P_HALO, L) for s in range(segs)]
        pos = start + (tile_index * n_sub + r) * L + lax.broadcasted_iota(jnp.int32, (L, 1), 0)
        dgs = []
        for gi, win in enumerate(POOL_WINDOWS):
            lanes = slice(gi * group, (gi + 1) * group)
            inv_cnt = 1.0 / jnp.minimum(pos + 1, win).astype(_F32)
            ds = [sums[s][gi] * inv_cnt - pp[s * L:(s + 1) * L, lanes] for s in range(segs)]
            dg = ds[0] if segs == 1 else jnp.concatenate(ds, axis=0)
            dgs.append(dg.astype(_BF16))
        for s in range(segs):
            zbuf[s, Z_HALO - CONV_HIST:Z_HALO, :] = zbuf[s, Z_HALO + L - CONV_HIST:Z_HALO + L, :]
            for buf in levels:
                buf[s, P_HALO - POOL_CARRY:P_HALO, :] = buf[s, P_HALO + L - POOL_CARRY:P_HALO + L, :]
        b_gate = w["b_gate"]
        g_a = jax.nn.sigmoid(pga + b_gate[:, :d])
        g_b = jax.nn.sigmoid(pgb + b_gate[:, d:])
        return ya, dgs, g_a, g_b

    def output(r, ya, dgs, g_a, g_b):
        branch_a = _dot(ya, w["w_conv_out"][...])
        branch_b = _dot(jnp.concatenate(dgs, axis=-1), w["pool_fold"][...])
        merged = (g_a * branch_a + g_b * branch_b).astype(_BF16)
        o = _dot(merged, w["w_o"][...])
        rows = pl.ds(r * M, M)
        y_ref[rows, :] = x_ref[rows, :] + _rms_norm(o, w["g_post"])

    projections = project(0)
    for r in range(n_sub):
        upcoming = project(r + 1) if r + 1 < n_sub else None
        output(r, *mix(r, projections))
        projections = upcoming

    for s in range(segs):
        zstate_ref[s] = zbuf[s, Z_HALO - CONV_HIST:Z_HALO, :]
        pstate_ref[s] = levels[0][s, P_HALO - POOL_HIST:P_HALO, :]


def _mixer_kernel(xp_ref, xs_ref, zhist_ref, phist_ref, b_gate_ref, conv_w_ref,
                  pool_scale_ref, g_pre_ref, g_post_ref,
                  w_in_hbm, w_conv_out_hbm, pool_w_hbm, w_pool_out_hbm, w_o_hbm,
                  yp_ref, ys_ref, zstate_p_ref, pstate_p_ref, zstate_s_ref, pstate_s_ref,
                  w_in_v, w_conv_out_v, pool_fold_v, w_o_v,
                  zbuf_p, zbuf_s, *level_bufs, layer, n_prompt_tiles, dec_rows):
    i = pl.program_id(0)
    levels_p = level_bufs[:len(POOL_WINDOWS)]
    levels_s = level_bufs[len(POOL_WINDOWS):]

    def zero_rows(buf, n):
        buf[:, 0:n, :] = jnp.zeros((buf.shape[0], n, buf.shape[2]), _F32)

    @pl.when(i == 0)
    def _first_step():
        plain = _column_blocks(w_in_hbm.at[layer], w_in_v, w_conv_out_v.shape[1]) + [
            (w_conv_out_hbm.at[layer], w_conv_out_v), (w_o_hbm.at[layer], w_o_v)]
        _fold_pool_weights(pool_w_hbm.at[layer], w_pool_out_hbm.at[layer],
                           pool_scale_ref[layer:layer + 1, :], pool_fold_v,
                           lambda steps: _load_weights_as_bf16(plain, between_chunks=steps))
        zero_rows(zbuf_p, Z_HALO)
        for buf in levels_p:
            zero_rows(buf, P_HALO)

    w = dict(
        w_in=w_in_v, w_conv_out=w_conv_out_v, pool_fold=pool_fold_v, w_o=w_o_v,
        b_gate=b_gate_ref[layer:layer + 1, :], conv_w=conv_w_ref[layer],
        g_pre=g_pre_ref[layer:layer + 1, :], g_post=g_post_ref[layer:layer + 1, :])

    @pl.when(i < n_prompt_tiles)
    def _prompt_tile():
        _mixer_tile(xp_ref, yp_ref, zstate_p_ref, pstate_p_ref, zbuf_p, levels_p, w,
                    tile_index=i, segs=1, seg_rows=PROMPT_SUB_ROWS, n_sub=MIXER_SUBTILES,
                    start=0)

    @pl.when(i == n_prompt_tiles)
    def _decode_rows():
        segs = zhist_ref.shape[0]
        zbuf_s[:, Z_HALO - CONV_HIST:Z_HALO, :] = zhist_ref[...]
        for buf in levels_s:
            zero_rows(buf, P_HALO)
        levels_s[0][:, P_HALO - POOL_HIST:P_HALO, :] = phist_ref[...]
        for s in range(segs):
            _pool_window_sums(levels_s, s, P_HALO - POOL_CARRY, POOL_CARRY)
        _mixer_tile(xs_ref, ys_ref, zstate_s_ref, pstate_s_ref, zbuf_s, levels_s, w,
                    tile_index=0, segs=segs, seg_rows=dec_rows, n_sub=1, start=PAST_LEN)


def _ffn_rows(x_ref, y_ref, w_up, w_down, g_pre, g_post, *, sub_rows, chunk):
    n_sub = x_ref.shape[0] // sub_rows
    n_chunks = w_up.shape[-1] // chunk
    hbs = [_rms_norm(x_ref[pl.ds(r * sub_rows, sub_rows), :], g_pre).astype(_BF16)
           for r in range(n_sub)]
    fs = [None] * n_sub
    for k in range(n_chunks):
        for r in range(n_sub):
            u = _dot(hbs[r], w_up[:, k * chunk:(k + 1) * chunk])
            a = jnp.square(jnp.maximum(u, 0.0)).astype(_BF16)
            part = _dot(a, w_down[k * chunk:(k + 1) * chunk, :])
            fs[r] = part if fs[r] is None else fs[r] + part
    for r in range(n_sub):
        rows = pl.ds(r * sub_rows, sub_rows)
        y_ref[rows, :] = x_ref[rows, :] + _rms_norm(fs[r], g_post)


def _ffn_kernel(xp_ref, xs_ref, g_pre_ref, g_post_ref, w_up_hbm, w_down_hbm,
                yp_ref, ys_ref, w_up_v, w_down_v, *, layer, n_prompt_tiles):
    i = pl.program_id(0)
    d = xp_ref.shape[-1]

    @pl.when(i == 0)
    def _first_step():
        _load_weights_as_bf16(_column_blocks(w_up_hbm.at[layer], w_up_v, d)
                              + [(w_down_hbm.at[layer], w_down_v)])

    g_pre = g_pre_ref[layer:layer + 1, :]
    g_post = g_post_ref[layer:layer + 1, :]

    @pl.when(i < n_prompt_tiles)
    def _prompt_tile():
        _ffn_rows(xp_ref, yp_ref, w_up_v, w_down_v, g_pre, g_post,
                  sub_rows=PROMPT_SUB_ROWS, chunk=d)

    @pl.when(i == n_prompt_tiles)
    def _decode_rows():
        _ffn_rows(xs_ref, ys_ref, w_up_v, w_down_v, g_pre, g_post,
                  sub_rows=xs_ref.shape[0], chunk=d)


def _full_spec(shape):
    zeros = (0,) * len(shape)
    return pl.BlockSpec(shape, lambda i: zeros)


_ANY_SPEC = pl.BlockSpec(memory_space=pl.ANY)
_COMPILER_PARAMS = pltpu.CompilerParams(
    dimension_semantics=("arbitrary",), vmem_limit_bytes=VMEM_LIMIT_BYTES)


def _carry_buffers(segs, rows, d):
    group = d // len(POOL_WINDOWS)
    return ([pltpu.VMEM((segs, Z_HALO + rows, d), _F32)],
            [pltpu.VMEM((segs, P_HALO + rows, d - k * group), _F32)
             for k in range(len(POOL_WINDOWS))])


def _mixer_call(xp, xs, zhist, phist, p, *, layer, dec_rows):
    seq, d = xp.shape
    tile = MIXER_SUBTILES * PROMPT_SUB_ROWS
    n_tiles = seq // tile
    segs = zhist.shape[0]
    small = (p["b_gate"], p["conv_w"], p["pool_scale"], p["g_mix_pre"], p["g_mix_post"])
    big = (p["w_in"], p["w_conv_out"], p["pool_w"], p["w_pool_out"], p["w_o"])
    prompt_spec = pl.BlockSpec((tile, d), lambda i: (jnp.minimum(i, n_tiles - 1), 0))
    f32 = xp.dtype
    zbuf_p, levels_p = _carry_buffers(1, PROMPT_SUB_ROWS, d)
    zbuf_s, levels_s = _carry_buffers(segs, dec_rows, d)
    kern = functools.partial(_mixer_kernel, layer=layer, n_prompt_tiles=n_tiles,
                             dec_rows=dec_rows)
    return pl.pallas_call(
        kern,
        grid=(n_tiles + 1,),
        in_specs=[prompt_spec, _full_spec(xs.shape), _full_spec(zhist.shape),
                  _full_spec(phist.shape)]
                 + [_full_spec(a.shape) for a in small] + [_ANY_SPEC] * len(big),
        out_specs=[prompt_spec, _full_spec(xs.shape),
                   _full_spec((1, CONV_HIST, d)), _full_spec((1, POOL_HIST, d)),
                   _full_spec(zhist.shape), _full_spec(phist.shape)],
        out_shape=[jax.ShapeDtypeStruct(xp.shape, f32), jax.ShapeDtypeStruct(xs.shape, f32),
                   jax.ShapeDtypeStruct((1, CONV_HIST, d), f32),
                   jax.ShapeDtypeStruct((1, POOL_HIST, d), f32),
                   jax.ShapeDtypeStruct(zhist.shape, f32),
                   jax.ShapeDtypeStruct(phist.shape, f32)],
        scratch_shapes=[pltpu.VMEM(p[name].shape[1:], _BF16)
                        for name in ("w_in", "w_conv_out", "w_pool_out", "w_o")]
                       + zbuf_p + zbuf_s + levels_p + levels_s,
        compiler_params=_COMPILER_PARAMS,
        name=f"mixer_l{layer}",
    )(xp, xs, zhist, phist, *small, *big)


def _ffn_call(xp, xs, p, *, layer):
    seq, d = xp.shape
    tile = FFN_SUBTILES * PROMPT_SUB_ROWS
    n_tiles = seq // tile
    small = (p["g_ffn_pre"], p["g_ffn_post"])
    big = (p["w_up"], p["w_down"])
    prompt_spec = pl.BlockSpec((tile, d), lambda i: (jnp.minimum(i, n_tiles - 1), 0))
    return pl.pallas_call(
        functools.partial(_ffn_kernel, layer=layer, n_prompt_tiles=n_tiles),
        grid=(n_tiles + 1,),
        in_specs=[prompt_spec, _full_spec(xs.shape)]
                 + [_full_spec(a.shape) for a in small] + [_ANY_SPEC] * len(big),
        out_specs=[prompt_spec, _full_spec(xs.shape)],
        out_shape=[jax.ShapeDtypeStruct(xp.shape, xp.dtype),
                   jax.ShapeDtypeStruct(xs.shape, xs.dtype)],
        scratch_shapes=[pltpu.VMEM(a.shape[1:], _BF16) for a in big],
        compiler_params=_COMPILER_PARAMS,
        name=f"ffn_l{layer}",
    )(xp, xs, *small, *big)


def kernel(x_prompt, x_sample, cache_conv, cache_pool, w_in, b_gate, conv_w, w_conv_out,
           pool_w, pool_scale, w_pool_out, w_o, g_mix_pre, g_mix_post,
           w_up, w_down, g_ffn_pre, g_ffn_post):
    depth, d = g_mix_pre.shape
    batch, seq, _ = x_prompt.shape
    dec_batch, dec_seq, _ = x_sample.shape
    assert batch == 1
    assert seq % (PROMPT_SUB_ROWS * MIXER_SUBTILES) == 0
    assert seq % (PROMPT_SUB_ROWS * FFN_SUBTILES) == 0
    assert dec_seq >= POOL_HIST and dec_seq % SUBLANES == 0

    n_groups, group = pool_w.shape[1], pool_w.shape[2]
    params = dict(
        w_in=w_in, b_gate=b_gate, conv_w=conv_w, w_conv_out=w_conv_out,
        pool_w=pool_w.reshape(depth, n_groups * group, group), pool_scale=pool_scale,
        w_pool_out=w_pool_out, w_o=w_o, g_mix_pre=g_mix_pre, g_mix_post=g_mix_post,
        w_up=w_up, w_down=w_down, g_ffn_pre=g_ffn_pre, g_ffn_post=g_ffn_post)

    xp = x_prompt.reshape(seq, d)
    xs = x_sample.reshape(dec_batch * dec_seq, d)
    conv_p, pool_p, conv_s, pool_s = [], [], [], []
    for l in range(depth):
        xp, xs, zp, pp, zs, ps = _mixer_call(xp, xs, cache_conv[l], cache_pool[l], params,
                                             layer=l, dec_rows=dec_seq)
        xp, xs = _ffn_call(xp, xs, params, layer=l)
        conv_p.append(zp)
        pool_p.append(pp)
        conv_s.append(zs)
        pool_s.append(ps)
    return (xp.reshape(x_prompt.shape), xs.reshape(x_sample.shape),
            jnp.stack(conv_p), jnp.stack(pool_p), jnp.stack(conv_s), jnp.stack(pool_s))
```

```python
import functools

import jax
import jax.numpy as jnp
from jax import lax
from jax.experimental import pallas as pl
from jax.experimental.pallas import tpu as pltpu

EPS = 1e-6
CONV_W = 3
CONV_HIST = CONV_W - 1
POOL_WINDOWS = (2, 4, 8, 16)
POOL_HIST = max(POOL_WINDOWS) - 1
N_PROJ = 6
N_STATES = 4
PAST_LEN = 2048

SUBLANES = 8
Z_HALO = -(-CONV_HIST // SUBLANES) * SUBLANES
POOL_CARRY = -(-max(POOL_WINDOWS) // SUBLANES) * SUBLANES
P_HALO = POOL_CARRY + SUBLANES
assert all(win == 2 ** (k + 1) for k, win in enumerate(POOL_WINDOWS))
assert POOL_HIST <= POOL_CARRY

PROMPT_SUB_ROWS = 256
MIXER_SUBTILES = 2
FFN_SUBTILES = 4
WEIGHT_STAGE_BYTES = 1024 * 1024
WEIGHT_COPIES_IN_FLIGHT = 7
BF16_SUBLANES = 2 * SUBLANES
ROUND_SLAB_ELEMS = 16 * 1024
VMEM_LIMIT_BYTES = 56 * 1024 * 1024

_F32 = jnp.float32
_BF16 = jnp.bfloat16


def _rms_norm(x, g):
    ms = jnp.mean(x * x, axis=-1, keepdims=True)
    return x * lax.rsqrt(ms + EPS) * g


def _dot(a, b):
    return jnp.dot(a, b, preferred_element_type=_F32)


def _stage_rows(n_rows, n_cols):
    rows = n_rows
    while rows * n_cols * 4 > WEIGHT_STAGE_BYTES and rows % (2 * SUBLANES) == 0:
        rows //= 2
    return rows


def _round_rows(src, dst):
    n_rows, n_cols = src.shape
    slab = BF16_SUBLANES
    while slab * n_cols < ROUND_SLAB_ELEMS and n_rows % (2 * slab) == 0:
        slab *= 2

    def round_slab(t, carry):
        rows = pl.ds(pl.multiple_of(t * slab, slab), slab)
        dst[rows, :] = src[rows, :].astype(_BF16)
        return carry

    lax.fori_loop(0, n_rows // slab, round_slab, 0)


def _column_blocks(src, dst, width):
    n_rows, n_cols = dst.shape
    return [(src.at[pl.ds(0, n_rows), pl.ds(c, width)], dst.at[pl.ds(0, n_rows), pl.ds(c, width)])
            for c in range(0, n_cols, width)]


def _load_weights_as_bf16(pairs, between_chunks=()):
    plan = []
    for src, _ in pairs:
        shape = (_stage_rows(*src.shape), src.shape[1])
        n_chunks = src.shape[0] // shape[0]
        for k, (other, count) in enumerate(plan):
            if other == shape:
                plan[k] = (shape, count + n_chunks)
                break
        else:
            plan.append((shape, n_chunks))
    shapes = [shape for shape, _ in plan]
    slots = [min(WEIGHT_COPIES_IN_FLIGHT + 1, count) for _, count in plan]

    def body(*scratch):
        stages, sems = scratch[:len(shapes)], scratch[len(shapes):]
        used = [0] * len(shapes)
        jobs = []
        for src, dst in pairs:
            rows = _stage_rows(*src.shape)
            which = shapes.index((rows, src.shape[1]))
            for k in range(src.shape[0] // rows):
                slot = used[which] % slots[which]
                used[which] += 1
                chunk = pl.ds(k * rows, rows)
                copy = pltpu.make_async_copy(
                    src.at[chunk], stages[which].at[slot], sems[which].at[slot])
                jobs.append((copy, stages[which], slot, dst, chunk))
        for copy, *_ in jobs[:WEIGHT_COPIES_IN_FLIGHT]:
            copy.start()
        stride = len(jobs) // (len(between_chunks) + 1)
        for j, (copy, stage, slot, dst, chunk) in enumerate(jobs):
            if j + WEIGHT_COPIES_IN_FLIGHT < len(jobs):
                jobs[j + WEIGHT_COPIES_IN_FLIGHT][0].start()
            copy.wait()
            _round_rows(stage.at[slot], dst.at[chunk])
            if (j + 1) % stride == 0 and (j + 1) // stride <= len(between_chunks):
                between_chunks[(j + 1) // stride - 1]()

    pl.run_scoped(body,
                  *[pltpu.VMEM((n,) + shape, _F32) for n, shape in zip(slots, shapes)],
                  *[pltpu.SemaphoreType.DMA((n,)) for n in slots])


def _dot_split3(a, b):
    a_hi, b_hi = a.astype(_BF16), b.astype(_BF16)
    a_lo = (a - a_hi.astype(_F32)).astype(_BF16)
    b_lo = (b - b_hi.astype(_F32)).astype(_BF16)
    return _dot(a_hi, b_hi) + _dot(a_hi, b_lo) + _dot(a_lo, b_hi)


def _fold_pool_weights(pool_w_hbm, w_pool_out_hbm, scale, dst, load_other_weights):
    n_groups = len(POOL_WINDOWS)
    group = pool_w_hbm.shape[-1]

    def body(pw_stage, out_stage, sems):
        copies = [pltpu.make_async_copy(pool_w_hbm, pw_stage, sems.at[0])]
        for g in range(n_groups):
            copies.append(pltpu.make_async_copy(
                w_pool_out_hbm.at[pl.ds(g * group, group)], out_stage.at[g], sems.at[1 + g]))
        for copy in copies:
            copy.start()

        def fold_group(g):
            if g == 0:
                copies[0].wait()
            copies[1 + g].wait()
            rows = pl.ds(g * group, group)
            scaled = pw_stage[rows, :] * scale[:, g * group:(g + 1) * group]
            dst[rows, :] = _dot_split3(scaled, out_stage[g]).astype(_BF16)

        load_other_weights([functools.partial(fold_group, g) for g in range(n_groups)])

    pl.run_scoped(body,
                  pltpu.VMEM(pool_w_hbm.shape, _F32),
                  pltpu.VMEM((n_groups, group) + w_pool_out_hbm.shape[1:], _F32),
                  pltpu.SemaphoreType.DMA((1 + n_groups,)))


def _pool_window_sums(levels, s, row0, n):
    group = levels[0].shape[-1] // len(POOL_WINDOWS)
    acc = levels[0][s, row0:row0 + n, :]
    sums = []
    for k, win in enumerate(POOL_WINDOWS):
        shift = win // 2
        acc = acc + levels[k][s, row0 - shift:row0 - shift + n, :]
        sums.append(acc[:, :group])
        if k + 1 < len(POOL_WINDOWS):
            acc = acc[:, group:]
            levels[k + 1][s, row0:row0 + n, :] = acc
    return sums


def _mixer_tile(x_ref, y_ref, zstate_ref, pstate_ref, zbuf, levels, w, *,
                tile_index, segs, seg_rows, n_sub, start):
    d = x_ref.shape[-1]
    L = seg_rows
    M = segs * L
    group = d // len(POOL_WINDOWS)

    def project(r):
        hb = _rms_norm(x_ref[pl.ds(r * M, M), :], w["g_pre"]).astype(_BF16)
        return [_dot(hb, w["w_in"][:, k * d:(k + 1) * d]) for k in range(N_PROJ)]

    def mix(r, projections):
        pb, pc, pv, pp, pga, pgb = projections
        z = pc * pv
        for s in range(segs):
            zbuf[s, Z_HALO:Z_HALO + L, :] = z[s * L:(s + 1) * L]
        conv_w = w["conv_w"]
        ys = []
        for s in range(segs):
            y = conv_w[CONV_W - 1:CONV_W] * z[s * L:(s + 1) * L]
            for k in range(CONV_W - 1):
                off = Z_HALO - (CONV_W - 1 - k)
                y = y + conv_w[k:k + 1] * zbuf[s, off:off + L, :]
            ys.append(y)
        y = ys[0] if segs == 1 else jnp.concatenate(ys, axis=0)
        ya = (pb * y).astype(_BF16)
        for s in range(segs):
            levels[0][s, P_HALO:P_HALO + L, :] = pp[s * L:(s + 1) * L]
        sums = [_pool_window_sums(levels, s, P_HALO, L) for s in range(segs)]
        pos = start + (tile_index * n_sub + r) * L + lax.broadcasted_iota(jnp.int32, (L, 1), 0)
        dgs = []
        for gi, win in enumerate(POOL_WINDOWS):
            lanes = slice(gi * group, (gi + 1) * group)
            inv_cnt = 1.0 / jnp.minimum(pos + 1, win).astype(_F32)
            ds = [sums[s][gi] * inv_cnt - pp[s * L:(s + 1) * L, lanes] for s in range(segs)]
            dg = ds[0] if segs == 1 else jnp.concatenate(ds, axis=0)
            dgs.append(dg.astype(_BF16))
        for s in range(segs):
            zbuf[s, Z_HALO - CONV_HIST:Z_HALO, :] = zbuf[s, Z_HALO + L - CONV_HIST:Z_HALO + L, :]
            for buf in levels:
                buf[s, P_HALO - POOL_CARRY:P_HALO, :] = buf[s, P_HALO + L - POOL_CARRY:P_HALO + L, :]
        b_gate = w["b_gate"]
        g_a = jax.nn.sigmoid(pga + b_gate[:, :d])
        g_b = jax.nn.sigmoid(pgb + b_gate[:, d:])
        return ya, dgs, g_a, g_b

    def output(r, ya, dgs, g_a, g_b):
        branch_a = _dot(ya, w["w_conv_out"][...])
        branch_b = _dot(jnp.concatenate(dgs, axis=-1), w["pool_fold"][...])
        merged = (g_a * branch_a + g_b * branch_b).astype(_BF16)
        o = _dot(merged, w["w_o"][...])
        rows = pl.ds(r * M, M)
        y_ref[rows, :] = x_ref[rows, :] + _rms_norm(o, w["g_post"])

    projections = project(0)
    for r in range(n_sub):
        upcoming = project(r + 1) if r + 1 < n_sub else None
        output(r, *mix(r, projections))
        projections = upcoming

    for s in range(segs):
        zstate_ref[s] = zbuf[s, Z_HALO - CONV_HIST:Z_HALO, :]
        pstate_ref[s] = levels[0][s, P_HALO - POOL_HIST:P_HALO, :]


def _mixer_kernel(*refs, layer, n_prompt_tiles, dec_rows):
    refs = list(refs)

    def take(n):
        head, refs[:] = refs[:n], refs[n:]
        return head

    (xp_ref, xs_ref, zhist_ref, phist_ref, b_gate_ref, conv_w_ref, pool_scale_ref,
     g_pre_ref, g_post_ref) = take(9)
    earlier_states = take(N_STATES if layer else 0)
    w_in_hbm, w_conv_out_hbm, pool_w_hbm, w_pool_out_hbm, w_o_hbm = take(5)
    yp_ref, ys_ref = take(2)
    states = take(N_STATES)
    w_in_v, w_conv_out_v, pool_fold_v, w_o_v = take(4)
    zbuf_p, zbuf_s = take(2)
    levels_p, levels_s = take(len(POOL_WINDOWS)), take(len(POOL_WINDOWS))
    zstate_p_ref, pstate_p_ref, zstate_s_ref, pstate_s_ref = [ref.at[layer] for ref in states]
    i = pl.program_id(0)

    def zero_rows(buf, n):
        buf[:, 0:n, :] = jnp.zeros((buf.shape[0], n, buf.shape[2]), _F32)

    @pl.when(i == 0)
    def _first_step():
        plain = _column_blocks(w_in_hbm.at[layer], w_in_v, w_conv_out_v.shape[1]) + [
            (w_conv_out_hbm.at[layer], w_conv_out_v), (w_o_hbm.at[layer], w_o_v)]
        _fold_pool_weights(pool_w_hbm.at[layer], w_pool_out_hbm.at[layer],
                           pool_scale_ref[layer:layer + 1, :], pool_fold_v,
                           lambda steps: _load_weights_as_bf16(plain, between_chunks=steps))
        zero_rows(zbuf_p, Z_HALO)
        for buf in levels_p:
            zero_rows(buf, P_HALO)
        for earlier, stacked in zip(earlier_states, states):
            stacked[0:layer] = earlier[...]

    w = dict(
        w_in=w_in_v, w_conv_out=w_conv_out_v, pool_fold=pool_fold_v, w_o=w_o_v,
        b_gate=b_gate_ref[layer:layer + 1, :], conv_w=conv_w_ref[layer],
        g_pre=g_pre_ref[layer:layer + 1, :], g_post=g_post_ref[layer:layer + 1, :])

    @pl.when(i < n_prompt_tiles)
    def _prompt_tile():
        _mixer_tile(xp_ref, yp_ref, zstate_p_ref, pstate_p_ref, zbuf_p, levels_p, w,
                    tile_index=i, segs=1, seg_rows=PROMPT_SUB_ROWS, n_sub=MIXER_SUBTILES,
                    start=0)

    @pl.when(i == n_prompt_tiles)
    def _decode_rows():
        segs = zhist_ref.shape[0]
        zbuf_s[:, Z_HALO - CONV_HIST:Z_HALO, :] = zhist_ref[...]
        for buf in levels_s:
            zero_rows(buf, P_HALO)
        levels_s[0][:, P_HALO - POOL_HIST:P_HALO, :] = phist_ref[...]
        for s in range(segs):
            _pool_window_sums(levels_s, s, P_HALO - POOL_CARRY, POOL_CARRY)
        _mixer_tile(xs_ref, ys_ref, zstate_s_ref, pstate_s_ref, zbuf_s, levels_s, w,
                    tile_index=0, segs=segs, seg_rows=dec_rows, n_sub=1, start=PAST_LEN)


def _ffn_rows(x_ref, y_ref, w_up, w_down, g_pre, g_post, *, sub_rows, chunk):
    n_sub = x_ref.shape[0] // sub_rows
    n_chunks = w_up.shape[-1] // chunk
    hbs = [_rms_norm(x_ref[pl.ds(r * sub_rows, sub_rows), :], g_pre).astype(_BF16)
           for r in range(n_sub)]
    fs = [None] * n_sub
    for k in range(n_chunks):
        for r in range(n_sub):
            u = _dot(hbs[r], w_up[:, k * chunk:(k + 1) * chunk])
            a = jnp.square(jnp.maximum(u, 0.0)).astype(_BF16)
            part = _dot(a, w_down[k * chunk:(k + 1) * chunk, :])
            fs[r] = part if fs[r] is None else fs[r] + part
    for r in range(n_sub):
        rows = pl.ds(r * sub_rows, sub_rows)
        y_ref[rows, :] = x_ref[rows, :] + _rms_norm(fs[r], g_post)


def _ffn_kernel(xp_ref, xs_ref, g_pre_ref, g_post_ref, w_up_hbm, w_down_hbm,
                yp_ref, ys_ref, w_up_v, w_down_v, *, layer, n_prompt_tiles):
    i = pl.program_id(0)
    d = xp_ref.shape[-1]

    @pl.when(i == 0)
    def _first_step():
        _load_weights_as_bf16(_column_blocks(w_up_hbm.at[layer], w_up_v, d)
                              + [(w_down_hbm.at[layer], w_down_v)])

    g_pre = g_pre_ref[layer:layer + 1, :]
    g_post = g_post_ref[layer:layer + 1, :]

    @pl.when(i < n_prompt_tiles)
    def _prompt_tile():
        _ffn_rows(xp_ref, yp_ref, w_up_v, w_down_v, g_pre, g_post,
                  sub_rows=PROMPT_SUB_ROWS, chunk=d)

    @pl.when(i == n_prompt_tiles)
    def _decode_rows():
        _ffn_rows(xs_ref, ys_ref, w_up_v, w_down_v, g_pre, g_post,
                  sub_rows=xs_ref.shape[0], chunk=d)


def _full_spec(shape):
    zeros = (0,) * len(shape)
    return pl.BlockSpec(shape, lambda i: zeros)


_ANY_SPEC = pl.BlockSpec(memory_space=pl.ANY)
_COMPILER_PARAMS = pltpu.CompilerParams(
    dimension_semantics=("arbitrary",), vmem_limit_bytes=VMEM_LIMIT_BYTES)


def _carry_buffers(segs, rows, d):
    group = d // len(POOL_WINDOWS)
    return ([pltpu.VMEM((segs, Z_HALO + rows, d), _F32)],
            [pltpu.VMEM((segs, P_HALO + rows, d - k * group), _F32)
             for k in range(len(POOL_WINDOWS))])


def _mixer_call(xp, xs, cache_conv, cache_pool, earlier_states, p, *, layer, dec_rows):
    seq, d = xp.shape
    tile = MIXER_SUBTILES * PROMPT_SUB_ROWS
    n_tiles = seq // tile
    segs = cache_conv.shape[1]
    small = (p["b_gate"], p["conv_w"], p["pool_scale"], p["g_mix_pre"], p["g_mix_post"])
    big = (p["w_in"], p["w_conv_out"], p["pool_w"], p["w_pool_out"], p["w_o"])
    prompt_spec = pl.BlockSpec((tile, d), lambda i: (jnp.minimum(i, n_tiles - 1), 0))

    def layer_spec(a):
        return pl.BlockSpec((None,) + a.shape[1:], lambda i: (layer, 0, 0, 0))

    f32 = xp.dtype
    state_shapes = [(layer + 1, 1, CONV_HIST, d), (layer + 1, 1, POOL_HIST, d),
                    (layer + 1, segs, CONV_HIST, d), (layer + 1, segs, POOL_HIST, d)]
    assert len(state_shapes) == N_STATES and len(earlier_states) == (N_STATES if layer else 0)
    zbuf_p, levels_p = _carry_buffers(1, PROMPT_SUB_ROWS, d)
    zbuf_s, levels_s = _carry_buffers(segs, dec_rows, d)
    kern = functools.partial(_mixer_kernel, layer=layer, n_prompt_tiles=n_tiles,
                             dec_rows=dec_rows)
    return pl.pallas_call(
        kern,
        grid=(n_tiles + 1,),
        in_specs=[prompt_spec, _full_spec(xs.shape), layer_spec(cache_conv),
                  layer_spec(cache_pool)]
                 + [_full_spec(a.shape) for a in small]
                 + [_full_spec(a.shape) for a in earlier_states] + [_ANY_SPEC] * len(big),
        out_specs=[prompt_spec, _full_spec(xs.shape)]
                  + [_full_spec(shape) for shape in state_shapes],
        out_shape=[jax.ShapeDtypeStruct(xp.shape, f32), jax.ShapeDtypeStruct(xs.shape, f32)]
                  + [jax.ShapeDtypeStruct(shape, f32) for shape in state_shapes],
        scratch_shapes=[pltpu.VMEM(p[name].shape[1:], _BF16)
                        for name in ("w_in", "w_conv_out", "w_pool_out", "w_o")]
                       + zbuf_p + zbuf_s + levels_p + levels_s,
        compiler_params=_COMPILER_PARAMS,
        name=f"mixer_l{layer}",
    )(xp, xs, cache_conv, cache_pool, *small, *earlier_states, *big)


def _ffn_call(xp, xs, p, *, layer):
    seq, d = xp.shape
    tile = FFN_SUBTILES * PROMPT_SUB_ROWS
    n_tiles = seq // tile
    small = (p["g_ffn_pre"], p["g_ffn_post"])
    big = (p["w_up"], p["w_down"])
    prompt_spec = pl.BlockSpec((tile, d), lambda i: (jnp.minimum(i, n_tiles - 1), 0))
    return pl.pallas_call(
        functools.partial(_ffn_kernel, layer=layer, n_prompt_tiles=n_tiles),
        grid=(n_tiles + 1,),
        in_specs=[prompt_spec, _full_spec(xs.shape)]
                 + [_full_spec(a.shape) for a in small] + [_ANY_SPEC] * len(big),
        out_specs=[prompt_spec, _full_spec(xs.shape)],
        out_shape=[jax.ShapeDtypeStruct(xp.shape, xp.dtype),
                   jax.ShapeDtypeStruct(xs.shape, xs.dtype)],
        scratch_shapes=[pltpu.VMEM(a.shape[1:], _BF16) for a in big],
        compiler_params=_COMPILER_PARAMS,
        name=f"ffn_l{layer}",
    )(xp, xs, *small, *big)


def kernel(x_prompt, x_sample, cache_conv, cache_pool, w_in, b_gate, conv_w, w_conv_out,
           pool_w, pool_scale, w_pool_out, w_o, g_mix_pre, g_mix_post,
           w_up, w_down, g_ffn_pre, g_ffn_post):
    depth, d = g_mix_pre.shape
    batch, seq, _ = x_prompt.shape
    dec_batch, dec_seq, _ = x_sample.shape
    assert batch == 1
    assert seq % (PROMPT_SUB_ROWS * MIXER_SUBTILES) == 0
    assert seq % (PROMPT_SUB_ROWS * FFN_SUBTILES) == 0
    assert dec_seq >= POOL_HIST and dec_seq % SUBLANES == 0

    n_groups, group = pool_w.shape[1], pool_w.shape[2]
    params = dict(
        w_in=w_in, b_gate=b_gate, conv_w=conv_w, w_conv_out=w_conv_out,
        pool_w=pool_w.reshape(depth, n_groups * group, group), pool_scale=pool_scale,
        w_pool_out=w_pool_out, w_o=w_o, g_mix_pre=g_mix_pre, g_mix_post=g_mix_post,
        w_up=w_up, w_down=w_down, g_ffn_pre=g_ffn_pre, g_ffn_post=g_ffn_post)

    xp = x_prompt.reshape(seq, d)
    xs = x_sample.reshape(dec_batch * dec_seq, d)
    states = []
    for l in range(depth):
        xp, xs, *states = _mixer_call(xp, xs, cache_conv, cache_pool, states, params,
                                      layer=l, dec_rows=dec_seq)
        xp, xs = _ffn_call(xp, xs, params, layer=l)
    return (xp.reshape(x_prompt.shape), xs.reshape(x_sample.shape), *states)
```

```python
import functools

import jax
import jax.numpy as jnp
from jax import lax
from jax.experimental import pallas as pl
from jax.experimental.pallas import tpu as pltpu

EPS = 1e-6
CONV_W = 3
CONV_HIST = CONV_W - 1
POOL_WINDOWS = (2, 4, 8, 16)
POOL_HIST = max(POOL_WINDOWS) - 1
N_PROJ = 6
N_STATES = 4
PAST_LEN = 2048

SUBLANES = 8
Z_HALO = -(-CONV_HIST // SUBLANES) * SUBLANES
POOL_CARRY = -(-max(POOL_WINDOWS) // SUBLANES) * SUBLANES
P_HALO = POOL_CARRY + SUBLANES
assert all(win == 2 ** (k + 1) for k, win in enumerate(POOL_WINDOWS))
assert POOL_HIST <= POOL_CARRY

PROMPT_SUB_ROWS = 256
MIXER_SUBTILES = 2
FFN_SUBTILES = 4
WEIGHT_STAGE_BYTES = 1024 * 1024
WEIGHT_COPIES_IN_FLIGHT = 7
BF16_SUBLANES = 2 * SUBLANES
ROUND_SLAB_ELEMS = 16 * 1024
VMEM_LIMIT_BYTES = 56 * 1024 * 1024

_F32 = jnp.float32
_BF16 = jnp.bfloat16


def _rms_norm(x, g):
    ms = jnp.mean(x * x, axis=-1, keepdims=True)
    return x * lax.rsqrt(ms + EPS) * g


def _dot(a, b):
    return jnp.dot(a, b, preferred_element_type=_F32)


def _stage_rows(n_rows, n_cols):
    rows = n_rows
    while rows * n_cols * 4 > WEIGHT_STAGE_BYTES and rows % (2 * SUBLANES) == 0:
        rows //= 2
    return rows


def _round_rows(src, dst):
    n_rows, n_cols = src.shape
    slab = BF16_SUBLANES
    while slab * n_cols < ROUND_SLAB_ELEMS and n_rows % (2 * slab) == 0:
        slab *= 2

    def round_slab(t, carry):
        rows = pl.ds(pl.multiple_of(t * slab, slab), slab)
        dst[rows, :] = src[rows, :].astype(_BF16)
        return carry

    lax.fori_loop(0, n_rows // slab, round_slab, 0)


def _column_blocks(src, dst, width):
    n_rows, n_cols = dst.shape
    return [(src.at[pl.ds(0, n_rows), pl.ds(c, width)], dst.at[pl.ds(0, n_rows), pl.ds(c, width)])
            for c in range(0, n_cols, width)]


def _load_weights_as_bf16(pairs, between_chunks=()):
    plan = []
    for src, _ in pairs:
        shape = (_stage_rows(*src.shape), src.shape[1])
        n_chunks = src.shape[0] // shape[0]
        for k, (other, count) in enumerate(plan):
            if other == shape:
                plan[k] = (shape, count + n_chunks)
                break
        else:
            plan.append((shape, n_chunks))
    shapes = [shape for shape, _ in plan]
    slots = [min(WEIGHT_COPIES_IN_FLIGHT + 1, count) for _, count in plan]

    def body(*scratch):
        stages, sems = scratch[:len(shapes)], scratch[len(shapes):]
        used = [0] * len(shapes)
        jobs = []
        for src, dst in pairs:
            rows = _stage_rows(*src.shape)
            which = shapes.index((rows, src.shape[1]))
            for k in range(src.shape[0] // rows):
                slot = used[which] % slots[which]
                used[which] += 1
                chunk = pl.ds(k * rows, rows)
                copy = pltpu.make_async_copy(
                    src.at[chunk], stages[which].at[slot], sems[which].at[slot])
                jobs.append((copy, stages[which], slot, dst, chunk))
        for copy, *_ in jobs[:WEIGHT_COPIES_IN_FLIGHT]:
            copy.start()
        stride = len(jobs) // (len(between_chunks) + 1)
        for j, (copy, stage, slot, dst, chunk) in enumerate(jobs):
            if j + WEIGHT_COPIES_IN_FLIGHT < len(jobs):
                jobs[j + WEIGHT_COPIES_IN_FLIGHT][0].start()
            copy.wait()
            _round_rows(stage.at[slot], dst.at[chunk])
            if (j + 1) % stride == 0 and (j + 1) // stride <= len(between_chunks):
                between_chunks[(j + 1) // stride - 1]()

    pl.run_scoped(body,
                  *[pltpu.VMEM((n,) + shape, _F32) for n, shape in zip(slots, shapes)],
                  *[pltpu.SemaphoreType.DMA((n,)) for n in slots])


def _dot_split3(a, b):
    a_hi, b_hi = a.astype(_BF16), b.astype(_BF16)
    a_lo = (a - a_hi.astype(_F32)).astype(_BF16)
    b_lo = (b - b_hi.astype(_F32)).astype(_BF16)
    return _dot(a_hi, b_hi) + _dot(a_hi, b_lo) + _dot(a_lo, b_hi)


def _fold_pool_weights(pool_w_hbm, w_pool_out_hbm, scale, dst, load_other_weights):
    n_groups = len(POOL_WINDOWS)
    group = pool_w_hbm.shape[-1]

    def body(pw_stage, out_stage, sems):
        copies = [pltpu.make_async_copy(pool_w_hbm, pw_stage, sems.at[0])]
        for g in range(n_groups):
            copies.append(pltpu.make_async_copy(
                w_pool_out_hbm.at[pl.ds(g * group, group)], out_stage.at[g], sems.at[1 + g]))
        for copy in copies:
            copy.start()

        def fold_group(g):
            if g == 0:
                copies[0].wait()
            copies[1 + g].wait()
            rows = pl.ds(g * group, group)
            scaled = pw_stage[rows, :] * scale[:, g * group:(g + 1) * group]
            dst[rows, :] = _dot_split3(scaled, out_stage[g]).astype(_BF16)

        load_other_weights([functools.partial(fold_group, g) for g in range(n_groups)])

    pl.run_scoped(body,
                  pltpu.VMEM(pool_w_hbm.shape, _F32),
                  pltpu.VMEM((n_groups, group) + w_pool_out_hbm.shape[1:], _F32),
                  pltpu.SemaphoreType.DMA((1 + n_groups,)))


def _pool_window_sums(levels, s, row0, n):
    group = levels[0].shape[-1] // len(POOL_WINDOWS)
    acc = levels[0][s, row0:row0 + n, :]
    sums = []
    for k, win in enumerate(POOL_WINDOWS):
        shift = win // 2
        acc = acc + levels[k][s, row0 - shift:row0 - shift + n, :]
        sums.append(acc[:, :group])
        if k + 1 < len(POOL_WINDOWS):
            acc = acc[:, group:]
            levels[k + 1][s, row0:row0 + n, :] = acc
    return sums


def _mixer_tile(x_ref, y_ref, zstate_ref, pstate_ref, zbuf, levels, w, *,
                tile_index, segs, seg_rows, n_sub, start, pool_state_row_major=False):
    d = x_ref.shape[-1]
    L = seg_rows
    M = segs * L
    group = d // len(POOL_WINDOWS)

    def project(r):
        hb = _rms_norm(x_ref[pl.ds(r * M, M), :], w["g_pre"]).astype(_BF16)
        return [_dot(hb, w["w_in"][:, k * d:(k + 1) * d]) for k in range(N_PROJ)]

    def mix(r, projections):
        pb, pc, pv, pp, pga, pgb = projections
        z = pc * pv
        for s in range(segs):
            zbuf[s, Z_HALO:Z_HALO + L, :] = z[s * L:(s + 1) * L]
        conv_w = w["conv_w"]
        ys = []
        for s in range(segs):
            y = conv_w[CONV_W - 1:CONV_W] * z[s * L:(s + 1) * L]
            for k in range(CONV_W - 1):
                off = Z_HALO - (CONV_W - 1 - k)
                y = y + conv_w[k:k + 1] * zbuf[s, off:off + L, :]
            ys.append(y)
        y = ys[0] if segs == 1 else jnp.concatenate(ys, axis=0)
        ya = (pb * y).astype(_BF16)
        for s in range(segs):
            levels[0][s, P_HALO:P_HALO + L, :] = pp[s * L:(s + 1) * L]
        sums = [_pool_window_sums(levels, s, P_HALO, L) for s in range(segs)]
        pos = start + (tile_index * n_sub + r) * L + lax.broadcasted_iota(jnp.int32, (L, 1), 0)
        dgs = []
        for gi, win in enumerate(POOL_WINDOWS):
            lanes = slice(gi * group, (gi + 1) * group)
            inv_cnt = 1.0 / jnp.minimum(pos + 1, win).astype(_F32)
            ds = [sums[s][gi] * inv_cnt - pp[s * L:(s + 1) * L, lanes] for s in range(segs)]
            dg = ds[0] if segs == 1 else jnp.concatenate(ds, axis=0)
            dgs.append(dg.astype(_BF16))
        for s in range(segs):
            zbuf[s, Z_HALO - CONV_HIST:Z_HALO, :] = zbuf[s, Z_HALO + L - CONV_HIST:Z_HALO + L, :]
            for buf in levels:
                buf[s, P_HALO - POOL_CARRY:P_HALO, :] = buf[s, P_HALO + L - POOL_CARRY:P_HALO + L, :]
        b_gate = w["b_gate"]
        g_a = jax.nn.sigmoid(pga + b_gate[:, :d])
        g_b = jax.nn.sigmoid(pgb + b_gate[:, d:])
        return ya, dgs, g_a, g_b

    def output(r, ya, dgs, g_a, g_b):
        branch_a = _dot(ya, w["w_conv_out"][...])
        branch_b = _dot(jnp.concatenate(dgs, axis=-1), w["pool_fold"][...])
        merged = (g_a * branch_a + g_b * branch_b).astype(_BF16)
        o = _dot(merged, w["w_o"][...])
        rows = pl.ds(r * M, M)
        y_ref[rows, :] = x_ref[rows, :] + _rms_norm(o, w["g_post"])

    projections = project(0)
    for r in range(n_sub):
        upcoming = project(r + 1) if r + 1 < n_sub else None
        output(r, *mix(r, projections))
        projections = upcoming

    for s in range(segs):
        zstate_ref[s] = zbuf[s, Z_HALO - CONV_HIST:Z_HALO, :]
        if pool_state_row_major:
            for h in range(POOL_HIST):
                row = P_HALO - POOL_HIST + h
                pstate_ref[h, s:s + 1, :] = levels[0][s, row:row + 1, :]
        else:
            pstate_ref[s] = levels[0][s, P_HALO - POOL_HIST:P_HALO, :]


def _mixer_kernel(*refs, layer, n_prompt_tiles, dec_rows):
    refs = list(refs)

    def take(n):
        head, refs[:] = refs[:n], refs[n:]
        return head

    (xp_ref, xs_ref, zhist_ref, phist_ref, b_gate_ref, conv_w_ref, pool_scale_ref,
     g_pre_ref, g_post_ref) = take(9)
    earlier_states = take(N_STATES if layer else 0)
    w_in_hbm, w_conv_out_hbm, pool_w_hbm, w_pool_out_hbm, w_o_hbm = take(5)
    yp_ref, ys_ref = take(2)
    states = take(N_STATES)
    w_in_v, w_conv_out_v, pool_fold_v, w_o_v = take(4)
    zbuf_p, zbuf_s = take(2)
    levels_p, levels_s = take(len(POOL_WINDOWS)), take(len(POOL_WINDOWS))
    zstate_p_ref, pstate_p_ref, zstate_s_ref, pstate_s_ref = [ref.at[layer] for ref in states]
    i = pl.program_id(0)

    def zero_rows(buf, n):
        buf[:, 0:n, :] = jnp.zeros((buf.shape[0], n, buf.shape[2]), _F32)

    @pl.when(i == 0)
    def _first_step():
        plain = _column_blocks(w_in_hbm.at[layer], w_in_v, w_conv_out_v.shape[1]) + [
            (w_conv_out_hbm.at[layer], w_conv_out_v), (w_o_hbm.at[layer], w_o_v)]
        _fold_pool_weights(pool_w_hbm.at[layer], w_pool_out_hbm.at[layer],
                           pool_scale_ref[layer:layer + 1, :], pool_fold_v,
                           lambda steps: _load_weights_as_bf16(plain, between_chunks=steps))
        zero_rows(zbuf_p, Z_HALO)
        for buf in levels_p:
            zero_rows(buf, P_HALO)
        for earlier, stacked in zip(earlier_states, states):
            stacked[0:layer] = earlier[...]

    w = dict(
        w_in=w_in_v, w_conv_out=w_conv_out_v, pool_fold=pool_fold_v, w_o=w_o_v,
        b_gate=b_gate_ref[layer:layer + 1, :], conv_w=conv_w_ref[layer],
        g_pre=g_pre_ref[layer:layer + 1, :], g_post=g_post_ref[layer:layer + 1, :])

    @pl.when(i < n_prompt_tiles)
    def _prompt_tile():
        _mixer_tile(xp_ref, yp_ref, zstate_p_ref, pstate_p_ref, zbuf_p, levels_p, w,
                    tile_index=i, segs=1, seg_rows=PROMPT_SUB_ROWS, n_sub=MIXER_SUBTILES,
                    start=0)

    @pl.when(i == n_prompt_tiles)
    def _decode_rows():
        segs = zhist_ref.shape[0]
        zbuf_s[:, Z_HALO - CONV_HIST:Z_HALO, :] = zhist_ref[...]
        for buf in levels_s:
            zero_rows(buf, P_HALO)
        for s in range(segs):
            for h in range(POOL_HIST):
                row = P_HALO - POOL_HIST + h
                levels_s[0][s, row:row + 1, :] = phist_ref[h, s:s + 1, :]
        for s in range(segs):
            _pool_window_sums(levels_s, s, P_HALO - POOL_CARRY, POOL_CARRY)
        _mixer_tile(xs_ref, ys_ref, zstate_s_ref, pstate_s_ref, zbuf_s, levels_s, w,
                    tile_index=0, segs=segs, seg_rows=dec_rows, n_sub=1, start=PAST_LEN,
                    pool_state_row_major=True)


def _ffn_rows(x_ref, y_ref, w_up, w_down, g_pre, g_post, *, sub_rows, chunk):
    n_sub = x_ref.shape[0] // sub_rows
    n_chunks = w_up.shape[-1] // chunk
    hbs = [_rms_norm(x_ref[pl.ds(r * sub_rows, sub_rows), :], g_pre).astype(_BF16)
           for r in range(n_sub)]
    fs = [None] * n_sub
    for k in range(n_chunks):
        for r in range(n_sub):
            u = _dot(hbs[r], w_up[:, k * chunk:(k + 1) * chunk])
            a = jnp.square(jnp.maximum(u, 0.0)).astype(_BF16)
            part = _dot(a, w_down[k * chunk:(k + 1) * chunk, :])
            fs[r] = part if fs[r] is None else fs[r] + part
    for r in range(n_sub):
        rows = pl.ds(r * sub_rows, sub_rows)
        y_ref[rows, :] = x_ref[rows, :] + _rms_norm(fs[r], g_post)


def _ffn_kernel(xp_ref, xs_ref, g_pre_ref, g_post_ref, w_up_hbm, w_down_hbm,
                yp_ref, ys_ref, w_up_v, w_down_v, *, layer, n_prompt_tiles):
    i = pl.program_id(0)
    d = xp_ref.shape[-1]

    @pl.when(i == 0)
    def _first_step():
        _load_weights_as_bf16(_column_blocks(w_up_hbm.at[layer], w_up_v, d)
                              + [(w_down_hbm.at[layer], w_down_v)])

    g_pre = g_pre_ref[layer:layer + 1, :]
    g_post = g_post_ref[layer:layer + 1, :]

    @pl.when(i < n_prompt_tiles)
    def _prompt_tile():
        _ffn_rows(xp_ref, yp_ref, w_up_v, w_down_v, g_pre, g_post,
                  sub_rows=PROMPT_SUB_ROWS, chunk=d)

    @pl.when(i == n_prompt_tiles)
    def _decode_rows():
        _ffn_rows(xs_ref, ys_ref, w_up_v, w_down_v, g_pre, g_post,
                  sub_rows=xs_ref.shape[0], chunk=d)


def _full_spec(shape):
    zeros = (0,) * len(shape)
    return pl.BlockSpec(shape, lambda i: zeros)


_ANY_SPEC = pl.BlockSpec(memory_space=pl.ANY)
_COMPILER_PARAMS = pltpu.CompilerParams(
    dimension_semantics=("arbitrary",), vmem_limit_bytes=VMEM_LIMIT_BYTES)


def _carry_buffers(segs, rows, d):
    group = d // len(POOL_WINDOWS)
    return ([pltpu.VMEM((segs, Z_HALO + rows, d), _F32)],
            [pltpu.VMEM((segs, P_HALO + rows, d - k * group), _F32)
             for k in range(len(POOL_WINDOWS))])


def _mixer_call(xp, xs, cache_conv, cache_pool, earlier_states, p, *, layer, dec_rows):
    seq, d = xp.shape
    tile = MIXER_SUBTILES * PROMPT_SUB_ROWS
    n_tiles = seq // tile
    segs = cache_conv.shape[1]
    small = (p["b_gate"], p["conv_w"], p["pool_scale"], p["g_mix_pre"], p["g_mix_post"])
    big = (p["w_in"], p["w_conv_out"], p["pool_w"], p["w_pool_out"], p["w_o"])
    prompt_spec = pl.BlockSpec((tile, d), lambda i: (jnp.minimum(i, n_tiles - 1), 0))

    def layer_spec(a):
        return pl.BlockSpec((None,) + a.shape[1:], lambda i: (layer, 0, 0, 0))

    f32 = xp.dtype
    state_shapes = [(layer + 1, 1, CONV_HIST, d), (layer + 1, 1, POOL_HIST, d),
                    (layer + 1, segs, CONV_HIST, d), (layer + 1, POOL_HIST, segs, d)]
    assert len(state_shapes) == N_STATES and len(earlier_states) == (N_STATES if layer else 0)
    zbuf_p, levels_p = _carry_buffers(1, PROMPT_SUB_ROWS, d)
    zbuf_s, levels_s = _carry_buffers(segs, dec_rows, d)
    kern = functools.partial(_mixer_kernel, layer=layer, n_prompt_tiles=n_tiles,
                             dec_rows=dec_rows)
    return pl.pallas_call(
        kern,
        grid=(n_tiles + 1,),
        in_specs=[prompt_spec, _full_spec(xs.shape), layer_spec(cache_conv),
                  layer_spec(cache_pool)]
                 + [_full_spec(a.shape) for a in small]
                 + [_full_spec(a.shape) for a in earlier_states] + [_ANY_SPEC] * len(big),
        out_specs=[prompt_spec, _full_spec(xs.shape)]
                  + [_full_spec(shape) for shape in state_shapes],
        out_shape=[jax.ShapeDtypeStruct(xp.shape, f32), jax.ShapeDtypeStruct(xs.shape, f32)]
                  + [jax.ShapeDtypeStruct(shape, f32) for shape in state_shapes],
        scratch_shapes=[pltpu.VMEM(p[name].shape[1:], _BF16)
                        for name in ("w_in", "w_conv_out", "w_pool_out", "w_o")]
                       + zbuf_p + zbuf_s + levels_p + levels_s,
        compiler_params=_COMPILER_PARAMS,
        name=f"mixer_l{layer}",
    )(xp, xs, cache_conv, cache_pool, *small, *earlier_states, *big)


def _ffn_call(xp, xs, p, *, layer):
    seq, d = xp.shape
    tile = FFN_SUBTILES * PROMPT_SUB_ROWS
    n_tiles = seq // tile
    small = (p["g_ffn_pre"], p["g_ffn_post"])
    big = (p["w_up"], p["w_down"])
    prompt_spec = pl.BlockSpec((tile, d), lambda i: (jnp.minimum(i, n_tiles - 1), 0))
    return pl.pallas_call(
        functools.partial(_ffn_kernel, layer=layer, n_prompt_tiles=n_tiles),
        grid=(n_tiles + 1,),
        in_specs=[prompt_spec, _full_spec(xs.shape)]
                 + [_full_spec(a.shape) for a in small] + [_ANY_SPEC] * len(big),
        out_specs=[prompt_spec, _full_spec(xs.shape)],
        out_shape=[jax.ShapeDtypeStruct(xp.shape, xp.dtype),
                   jax.ShapeDtypeStruct(xs.shape, xs.dtype)],
        scratch_shapes=[pltpu.VMEM(a.shape[1:], _BF16) for a in big],
        compiler_params=_COMPILER_PARAMS,
        name=f"ffn_l{layer}",
    )(xp, xs, *small, *big)


def kernel(x_prompt, x_sample, cache_conv, cache_pool, w_in, b_gate, conv_w, w_conv_out,
           pool_w, pool_scale, w_pool_out, w_o, g_mix_pre, g_mix_post,
           w_up, w_down, g_ffn_pre, g_ffn_post):
    depth, d = g_mix_pre.shape
    batch, seq, _ = x_prompt.shape
    dec_batch, dec_seq, _ = x_sample.shape
    assert batch == 1
    assert seq % (PROMPT_SUB_ROWS * MIXER_SUBTILES) == 0
    assert seq % (PROMPT_SUB_ROWS * FFN_SUBTILES) == 0
    assert dec_seq >= POOL_HIST and dec_seq % SUBLANES == 0

    n_groups, group = pool_w.shape[1], pool_w.shape[2]
    params = dict(
        w_in=w_in, b_gate=b_gate, conv_w=conv_w, w_conv_out=w_conv_out,
        pool_w=pool_w.reshape(depth, n_groups * group, group), pool_scale=pool_scale,
        w_pool_out=w_pool_out, w_o=w_o, g_mix_pre=g_mix_pre, g_mix_post=g_mix_post,
        w_up=w_up, w_down=w_down, g_ffn_pre=g_ffn_pre, g_ffn_post=g_ffn_post)

    xp = x_prompt.reshape(seq, d)
    xs = x_sample.reshape(dec_batch * dec_seq, d)
    cache_pool = jnp.swapaxes(cache_pool, 1, 2)
    states = []
    for l in range(depth):
        xp, xs, *states = _mixer_call(xp, xs, cache_conv, cache_pool, states, params,
                                      layer=l, dec_rows=dec_seq)
        xp, xs = _ffn_call(xp, xs, params, layer=l)
    states[-1] = jnp.swapaxes(states[-1], 1, 2)
    return (xp.reshape(x_prompt.shape), xs.reshape(x_sample.shape), *states)
```

```python
import functools

import jax
import jax.numpy as jnp
from jax import lax
from jax.experimental import pallas as pl
from jax.experimental.pallas import tpu as pltpu

EPS = 1e-6
CONV_W = 3
CONV_HIST = CONV_W - 1
POOL_WINDOWS = (2, 4, 8, 16)
POOL_HIST = max(POOL_WINDOWS) - 1
N_PROJ = 6
N_STATES = 4
PAST_LEN = 2048

SUBLANES = 8
Z_HALO = -(-CONV_HIST // SUBLANES) * SUBLANES
POOL_CARRY = -(-max(POOL_WINDOWS) // SUBLANES) * SUBLANES
P_HALO = POOL_CARRY + SUBLANES
assert all(win == 2 ** (k + 1) for k, win in enumerate(POOL_WINDOWS))
assert POOL_HIST <= POOL_CARRY

PROMPT_SUB_ROWS = 256
MIXER_SUBTILES = 2
FFN_SUBTILES = 4
WEIGHT_STAGE_BYTES = 1024 * 1024
WEIGHT_COPIES_IN_FLIGHT = 7
BF16_SUBLANES = 2 * SUBLANES
ROUND_SLAB_ELEMS = 16 * 1024
VMEM_LIMIT_BYTES = 56 * 1024 * 1024

_F32 = jnp.float32
_BF16 = jnp.bfloat16


def _rms_norm(x, g):
    ms = jnp.mean(x * x, axis=-1, keepdims=True)
    return x * lax.rsqrt(ms + EPS) * g


def _dot(a, b):
    return jnp.dot(a, b, preferred_element_type=_F32)


def _stage_rows(n_rows, n_cols):
    rows = n_rows
    while rows * n_cols * 4 > WEIGHT_STAGE_BYTES and rows % (2 * SUBLANES) == 0:
        rows //= 2
    return rows


def _round_rows(src, dst):
    n_rows, n_cols = src.shape
    slab = BF16_SUBLANES
    while slab * n_cols < ROUND_SLAB_ELEMS and n_rows % (2 * slab) == 0:
        slab *= 2

    def round_slab(t, carry):
        rows = pl.ds(pl.multiple_of(t * slab, slab), slab)
        dst[rows, :] = src[rows, :].astype(_BF16)
        return carry

    lax.fori_loop(0, n_rows // slab, round_slab, 0)


def _column_blocks(src, dst, width):
    n_rows, n_cols = dst.shape
    return [(src.at[pl.ds(0, n_rows), pl.ds(c, width)], dst.at[pl.ds(0, n_rows), pl.ds(c, width)])
            for c in range(0, n_cols, width)]


def _load_weights_as_bf16(pairs, between_chunks=()):
    plan = []
    for src, _ in pairs:
        shape = (_stage_rows(*src.shape), src.shape[1])
        n_chunks = src.shape[0] // shape[0]
        for k, (other, count) in enumerate(plan):
            if other == shape:
                plan[k] = (shape, count + n_chunks)
                break
        else:
            plan.append((shape, n_chunks))
    shapes = [shape for shape, _ in plan]
    slots = [min(WEIGHT_COPIES_IN_FLIGHT + 1, count) for _, count in plan]

    def body(*scratch):
        stages, sems = scratch[:len(shapes)], scratch[len(shapes):]
        used = [0] * len(shapes)
        jobs = []
        for src, dst in pairs:
            rows = _stage_rows(*src.shape)
            which = shapes.index((rows, src.shape[1]))
            for k in range(src.shape[0] // rows):
                slot = used[which] % slots[which]
                used[which] += 1
                chunk = pl.ds(k * rows, rows)
                copy = pltpu.make_async_copy(
                    src.at[chunk], stages[which].at[slot], sems[which].at[slot])
                jobs.append((copy, stages[which], slot, dst, chunk))
        for copy, *_ in jobs[:WEIGHT_COPIES_IN_FLIGHT]:
            copy.start()
        stride = len(jobs) // (len(between_chunks) + 1)
        for j, (copy, stage, slot, dst, chunk) in enumerate(jobs):
            if j + WEIGHT_COPIES_IN_FLIGHT < len(jobs):
                jobs[j + WEIGHT_COPIES_IN_FLIGHT][0].start()
            copy.wait()
            _round_rows(stage.at[slot], dst.at[chunk])
            if (j + 1) % stride == 0 and (j + 1) // stride <= len(between_chunks):
                between_chunks[(j + 1) // stride - 1]()

    pl.run_scoped(body,
                  *[pltpu.VMEM((n,) + shape, _F32) for n, shape in zip(slots, shapes)],
                  *[pltpu.SemaphoreType.DMA((n,)) for n in slots])


def _dot_split3(a, b):
    a_hi, b_hi = a.astype(_BF16), b.astype(_BF16)
    a_lo = (a - a_hi.astype(_F32)).astype(_BF16)
    b_lo = (b - b_hi.astype(_F32)).astype(_BF16)
    return _dot(a_hi, b_hi) + _dot(a_hi, b_lo) + _dot(a_lo, b_hi)


def _fold_pool_weights(pool_w_hbm, w_pool_out_hbm, scale, dst, load_other_weights):
    n_groups = len(POOL_WINDOWS)
    group = pool_w_hbm.shape[-1]

    def body(pw_stage, out_stage, sems):
        copies = [pltpu.make_async_copy(pool_w_hbm, pw_stage, sems.at[0])]
        for g in range(n_groups):
            copies.append(pltpu.make_async_copy(
                w_pool_out_hbm.at[pl.ds(g * group, group)], out_stage.at[g], sems.at[1 + g]))
        for copy in copies:
            copy.start()

        def fold_group(g):
            if g == 0:
                copies[0].wait()
            copies[1 + g].wait()
            rows = pl.ds(g * group, group)
            scaled = pw_stage[rows, :] * scale[:, g * group:(g + 1) * group]
            dst[rows, :] = _dot_split3(scaled, out_stage[g]).astype(_BF16)

        load_other_weights([functools.partial(fold_group, g) for g in range(n_groups)])

    pl.run_scoped(body,
                  pltpu.VMEM(pool_w_hbm.shape, _F32),
                  pltpu.VMEM((n_groups, group) + w_pool_out_hbm.shape[1:], _F32),
                  pltpu.SemaphoreType.DMA((1 + n_groups,)))


def _pool_window_sums(levels, s, row0, n):
    group = levels[0].shape[-1] // len(POOL_WINDOWS)
    acc = levels[0][s, row0:row0 + n, :]
    sums = []
    for k, win in enumerate(POOL_WINDOWS):
        shift = win // 2
        acc = acc + levels[k][s, row0 - shift:row0 - shift + n, :]
        sums.append(acc[:, :group])
        if k + 1 < len(POOL_WINDOWS):
            acc = acc[:, group:]
            levels[k + 1][s, row0:row0 + n, :] = acc
    return sums


def _mixer_tile(x_ref, y_ref, zstate_ref, pstate_ref, zbuf, levels, w, *,
                tile_index, segs, seg_rows, n_sub, start):
    d = x_ref.shape[-1]
    L = seg_rows
    M = segs * L
    group = d // len(POOL_WINDOWS)

    def project(r):
        hb = _rms_norm(x_ref[pl.ds(r * M, M), :], w["g_pre"]).astype(_BF16)
        return [_dot(hb, w["w_in"][:, k * d:(k + 1) * d]) for k in range(N_PROJ)]

    def mix(r, projections):
        pb, pc, pv, pp, pga, pgb = projections
        z = pc * pv
        for s in range(segs):
            zbuf[s, Z_HALO:Z_HALO + L, :] = z[s * L:(s + 1) * L]
        conv_w = w["conv_w"]
        ys = []
        for s in range(segs):
            y = conv_w[CONV_W - 1] * z[s * L:(s + 1) * L]
            for k in range(CONV_W - 1):
                off = Z_HALO - (CONV_W - 1 - k)
                y = y + conv_w[k] * zbuf[s, off:off + L, :]
            ys.append(y)
        y = ys[0] if segs == 1 else jnp.concatenate(ys, axis=0)
        ya = (pb * y).astype(_BF16)
        for s in range(segs):
            levels[0][s, P_HALO:P_HALO + L, :] = pp[s * L:(s + 1) * L]
        sums = [_pool_window_sums(levels, s, P_HALO, L) for s in range(segs)]
        pos = start + (tile_index * n_sub + r) * L + lax.broadcasted_iota(jnp.int32, (L, 1), 0)
        dgs = []
        for gi, win in enumerate(POOL_WINDOWS):
            lanes = slice(gi * group, (gi + 1) * group)
            inv_cnt = 1.0 / jnp.minimum(pos + 1, win).astype(_F32)
            ds = [sums[s][gi] * inv_cnt - pp[s * L:(s + 1) * L, lanes] for s in range(segs)]
            dg = ds[0] if segs == 1 else jnp.concatenate(ds, axis=0)
            dgs.append(dg.astype(_BF16))
        for s in range(segs):
            zbuf[s, Z_HALO - CONV_HIST:Z_HALO, :] = zbuf[s, Z_HALO + L - CONV_HIST:Z_HALO + L, :]
            for buf in levels:
                buf[s, P_HALO - POOL_CARRY:P_HALO, :] = buf[s, P_HALO + L - POOL_CARRY:P_HALO + L, :]
        b_gate = w["b_gate"]
        g_a = jax.nn.sigmoid(pga + b_gate[:, :d])
        g_b = jax.nn.sigmoid(pgb + b_gate[:, d:])
        return ya, dgs, g_a, g_b

    def output(r, ya, dgs, g_a, g_b):
        branch_a = _dot(ya, w["w_conv_out"][...])
        branch_b = _dot(jnp.concatenate(dgs, axis=-1), w["pool_fold"][...])
        merged = (g_a * branch_a + g_b * branch_b).astype(_BF16)
        o = _dot(merged, w["w_o"][...])
        rows = pl.ds(r * M, M)
        y_ref[rows, :] = x_ref[rows, :] + _rms_norm(o, w["g_post"])

    projections = project(0)
    for r in range(n_sub):
        upcoming = project(r + 1) if r + 1 < n_sub else None
        output(r, *mix(r, projections))
        projections = upcoming

    for s in range(segs):
        zstate_ref[s] = zbuf[s, Z_HALO - CONV_HIST:Z_HALO, :]
        for h in range(POOL_HIST):
            row = P_HALO - POOL_HIST + h
            pstate_ref[h, s:s + 1, :] = levels[0][s, row:row + 1, :]


def _mixer_kernel(*refs, layer, n_prompt_tiles, dec_rows):
    refs = list(refs)

    def take(n):
        head, refs[:] = refs[:n], refs[n:]
        return head

    (xp_ref, xs_ref, zhist_ref, phist_ref, b_gate_ref, conv_w_ref, pool_scale_ref,
     g_pre_ref, g_post_ref) = take(9)
    earlier_states = take(N_STATES if layer else 0)
    w_in_hbm, w_conv_out_hbm, pool_w_hbm, w_pool_out_hbm, w_o_hbm = take(5)
    yp_ref, ys_ref = take(2)
    states = take(N_STATES)
    w_in_v, w_conv_out_v, pool_fold_v, w_o_v = take(4)
    zbuf_p, zbuf_s = take(2)
    levels_p, levels_s = take(len(POOL_WINDOWS)), take(len(POOL_WINDOWS))
    zstate_p_ref, pstate_p_ref, zstate_s_ref, pstate_s_ref = [ref.at[layer] for ref in states]
    i = pl.program_id(0)

    def zero_rows(buf, n):
        buf[:, 0:n, :] = jnp.zeros((buf.shape[0], n, buf.shape[2]), _F32)

    @pl.when(i == 0)
    def _first_step():
        plain = _column_blocks(w_in_hbm.at[layer], w_in_v, w_conv_out_v.shape[1]) + [
            (w_conv_out_hbm.at[layer], w_conv_out_v), (w_o_hbm.at[layer], w_o_v)]
        _fold_pool_weights(pool_w_hbm.at[layer], w_pool_out_hbm.at[layer],
                           pool_scale_ref[layer:layer + 1, :], pool_fold_v,
                           lambda steps: _load_weights_as_bf16(plain, between_chunks=steps))
        zero_rows(zbuf_p, Z_HALO)
        for buf in levels_p:
            zero_rows(buf, P_HALO)
        for earlier, stacked in zip(earlier_states, states):
            stacked[0:layer] = earlier[...]

    w = dict(
        w_in=w_in_v, w_conv_out=w_conv_out_v, pool_fold=pool_fold_v, w_o=w_o_v,
        b_gate=b_gate_ref[layer:layer + 1, :],
        conv_w=[conv_w_ref[k, layer:layer + 1, :] for k in range(CONV_W)],
        g_pre=g_pre_ref[layer:layer + 1, :], g_post=g_post_ref[layer:layer + 1, :])

    @pl.when(i < n_prompt_tiles)
    def _prompt_tile():
        _mixer_tile(xp_ref, yp_ref, zstate_p_ref, pstate_p_ref, zbuf_p, levels_p, w,
                    tile_index=i, segs=1, seg_rows=PROMPT_SUB_ROWS, n_sub=MIXER_SUBTILES,
                    start=0)

    @pl.when(i == n_prompt_tiles)
    def _decode_rows():
        segs = zhist_ref.shape[0]
        zbuf_s[:, Z_HALO - CONV_HIST:Z_HALO, :] = zhist_ref[...]
        for buf in levels_s:
            zero_rows(buf, P_HALO)
        for s in range(segs):
            for h in range(POOL_HIST):
                row = P_HALO - POOL_HIST + h
                levels_s[0][s, row:row + 1, :] = phist_ref[h, s:s + 1, :]
        for s in range(segs):
            _pool_window_sums(levels_s, s, P_HALO - POOL_CARRY, POOL_CARRY)
        _mixer_tile(xs_ref, ys_ref, zstate_s_ref, pstate_s_ref, zbuf_s, levels_s, w,
                    tile_index=0, segs=segs, seg_rows=dec_rows, n_sub=1, start=PAST_LEN)


def _ffn_rows(x_ref, y_ref, w_up, w_down, g_pre, g_post, *, sub_rows, chunk):
    n_sub = x_ref.shape[0] // sub_rows
    n_chunks = w_up.shape[-1] // chunk
    hbs = [_rms_norm(x_ref[pl.ds(r * sub_rows, sub_rows), :], g_pre).astype(_BF16)
           for r in range(n_sub)]
    fs = [None] * n_sub
    for k in range(n_chunks):
        for r in range(n_sub):
            u = _dot(hbs[r], w_up[:, k * chunk:(k + 1) * chunk])
            a = jnp.square(jnp.maximum(u, 0.0)).astype(_BF16)
            part = _dot(a, w_down[k * chunk:(k + 1) * chunk, :])
            fs[r] = part if fs[r] is None else fs[r] + part
    for r in range(n_sub):
        rows = pl.ds(r * sub_rows, sub_rows)
        y_ref[rows, :] = x_ref[rows, :] + _rms_norm(fs[r], g_post)


def _ffn_kernel(xp_ref, xs_ref, g_pre_ref, g_post_ref, w_up_hbm, w_down_hbm,
                yp_ref, ys_ref, w_up_v, w_down_v, *, layer, n_prompt_tiles):
    i = pl.program_id(0)
    d = xp_ref.shape[-1]

    @pl.when(i == 0)
    def _first_step():
        _load_weights_as_bf16(_column_blocks(w_up_hbm.at[layer], w_up_v, d)
                              + [(w_down_hbm.at[layer], w_down_v)])

    g_pre = g_pre_ref[layer:layer + 1, :]
    g_post = g_post_ref[layer:layer + 1, :]

    @pl.when(i < n_prompt_tiles)
    def _prompt_tile():
        _ffn_rows(xp_ref, yp_ref, w_up_v, w_down_v, g_pre, g_post,
                  sub_rows=PROMPT_SUB_ROWS, chunk=d)

    @pl.when(i == n_prompt_tiles)
    def _decode_rows():
        _ffn_rows(xs_ref, ys_ref, w_up_v, w_down_v, g_pre, g_post,
                  sub_rows=xs_ref.shape[0], chunk=d)


def _full_spec(shape):
    zeros = (0,) * len(shape)
    return pl.BlockSpec(shape, lambda i: zeros)


_ANY_SPEC = pl.BlockSpec(memory_space=pl.ANY)
_COMPILER_PARAMS = pltpu.CompilerParams(
    dimension_semantics=("arbitrary",), vmem_limit_bytes=VMEM_LIMIT_BYTES)


def _carry_buffers(segs, rows, d):
    group = d // len(POOL_WINDOWS)
    return ([pltpu.VMEM((segs, Z_HALO + rows, d), _F32)],
            [pltpu.VMEM((segs, P_HALO + rows, d - k * group), _F32)
             for k in range(len(POOL_WINDOWS))])


def _mixer_call(xp, xs, cache_conv, cache_pool, earlier_states, p, *, layer, dec_rows):
    seq, d = xp.shape
    tile = MIXER_SUBTILES * PROMPT_SUB_ROWS
    n_tiles = seq // tile
    segs = cache_conv.shape[1]
    small = (p["b_gate"], p["conv_w"], p["pool_scale"], p["g_mix_pre"], p["g_mix_post"])
    big = (p["w_in"], p["w_conv_out"], p["pool_w"], p["w_pool_out"], p["w_o"])
    prompt_spec = pl.BlockSpec((tile, d), lambda i: (jnp.minimum(i, n_tiles - 1), 0))

    def layer_spec(a):
        return pl.BlockSpec((None,) + a.shape[1:], lambda i: (layer, 0, 0, 0))

    f32 = xp.dtype
    state_shapes = [(layer + 1, 1, CONV_HIST, d), (layer + 1, POOL_HIST, 1, d),
                    (layer + 1, segs, CONV_HIST, d), (layer + 1, POOL_HIST, segs, d)]
    assert len(state_shapes) == N_STATES and len(earlier_states) == (N_STATES if layer else 0)
    zbuf_p, levels_p = _carry_buffers(1, PROMPT_SUB_ROWS, d)
    zbuf_s, levels_s = _carry_buffers(segs, dec_rows, d)
    kern = functools.partial(_mixer_kernel, layer=layer, n_prompt_tiles=n_tiles,
                             dec_rows=dec_rows)
    return pl.pallas_call(
        kern,
        grid=(n_tiles + 1,),
        in_specs=[prompt_spec, _full_spec(xs.shape), layer_spec(cache_conv),
                  layer_spec(cache_pool)]
                 + [_full_spec(a.shape) for a in small]
                 + [_full_spec(a.shape) for a in earlier_states] + [_ANY_SPEC] * len(big),
        out_specs=[prompt_spec, _full_spec(xs.shape)]
                  + [_full_spec(shape) for shape in state_shapes],
        out_shape=[jax.ShapeDtypeStruct(xp.shape, f32), jax.ShapeDtypeStruct(xs.shape, f32)]
                  + [jax.ShapeDtypeStruct(shape, f32) for shape in state_shapes],
        scratch_shapes=[pltpu.VMEM(p[name].shape[1:], _BF16)
                        for name in ("w_in", "w_conv_out", "w_pool_out", "w_o")]
                       + zbuf_p + zbuf_s + levels_p + levels_s,
        compiler_params=_COMPILER_PARAMS,
        name=f"mixer_l{layer}",
    )(xp, xs, cache_conv, cache_pool, *small, *earlier_states, *big)


def _ffn_call(xp, xs, p, *, layer):
    seq, d = xp.shape
    tile = FFN_SUBTILES * PROMPT_SUB_ROWS
    n_tiles = seq // tile
    small = (p["g_ffn_pre"], p["g_ffn_post"])
    big = (p["w_up"], p["w_down"])
    prompt_spec = pl.BlockSpec((tile, d), lambda i: (jnp.minimum(i, n_tiles - 1), 0))
    return pl.pallas_call(
        functools.partial(_ffn_kernel, layer=layer, n_prompt_tiles=n_tiles),
        grid=(n_tiles + 1,),
        in_specs=[prompt_spec, _full_spec(xs.shape)]
                 + [_full_spec(a.shape) for a in small] + [_ANY_SPEC] * len(big),
        out_specs=[prompt_spec, _full_spec(xs.shape)],
        out_shape=[jax.ShapeDtypeStruct(xp.shape, xp.dtype),
                   jax.ShapeDtypeStruct(xs.shape, xs.dtype)],
        scratch_shapes=[pltpu.VMEM(a.shape[1:], _BF16) for a in big],
        compiler_params=_COMPILER_PARAMS,
        name=f"ffn_l{layer}",
    )(xp, xs, *small, *big)


def kernel(x_prompt, x_sample, cache_conv, cache_pool, w_in, b_gate, conv_w, w_conv_out,
           pool_w, pool_scale, w_pool_out, w_o, g_mix_pre, g_mix_post,
           w_up, w_down, g_ffn_pre, g_ffn_post):
    depth, d = g_mix_pre.shape
    batch, seq, _ = x_prompt.shape
    dec_batch, dec_seq, _ = x_sample.shape
    assert batch == 1
    assert seq % (PROMPT_SUB_ROWS * MIXER_SUBTILES) == 0
    assert seq % (PROMPT_SUB_ROWS * FFN_SUBTILES) == 0
    assert dec_seq >= POOL_HIST and dec_seq % SUBLANES == 0

    n_groups, group = pool_w.shape[1], pool_w.shape[2]
    params = dict(
        w_in=w_in, b_gate=b_gate, conv_w=jnp.swapaxes(conv_w, 0, 1), w_conv_out=w_conv_out,
        pool_w=pool_w.reshape(depth, n_groups * group, group), pool_scale=pool_scale,
        w_pool_out=w_pool_out, w_o=w_o, g_mix_pre=g_mix_pre, g_mix_post=g_mix_post,
        w_up=w_up, w_down=w_down, g_ffn_pre=g_ffn_pre, g_ffn_post=g_ffn_post)

    xp = x_prompt.reshape(seq, d)
    xs = x_sample.reshape(dec_batch * dec_seq, d)
    cache_pool = jnp.swapaxes(cache_pool, 1, 2)
    states = []
    for l in range(depth):
        xp, xs, *states = _mixer_call(xp, xs, cache_conv, cache_pool, states, params,
                                      layer=l, dec_rows=dec_seq)
        xp, xs = _ffn_call(xp, xs, params, layer=l)
    states[1], states[3] = jnp.swapaxes(states[1], 1, 2), jnp.swapaxes(states[3], 1, 2)
    return (xp.reshape(x_prompt.shape), xs.reshape(x_sample.shape), *states)
```
